```python
import math
import jax
import jax.numpy as jnp
from jax import lax
import numpy as np

D_MODEL = 1024
BATCH = 8
SEQ = 4096
DEPTH = 1

PLE_DIM = 256
N_ATT_HEADS = 8
ATT_HEAD_DIM = 64
ATT_WIDTH = N_ATT_HEADS * ATT_HEAD_DIM
KV_RANK = 128
N_IDX_HEADS = 8
IDX_HEAD_DIM = 64
TOPK_MAX = 256
Q_BLOCK = 128
D_CONV = 512
CONV_GROUPS = 8
CONV_K = 3
D_FF = 2816
FFN_CONV_K = 3
N_BRANCHES = 2
LN_EPS = 1e-5
DEEPNORM_ALPHA = (2.0 * DEPTH) ** 0.25
DEEPNORM_BETA = (8.0 * DEPTH) ** -0.25
IN_SPLITS = (ATT_WIDTH, KV_RANK, N_IDX_HEADS * IDX_HEAD_DIM, IDX_HEAD_DIM, N_IDX_HEADS,
             D_CONV, D_CONV, D_CONV, D_MODEL, D_MODEL)
IN_WIDTH = sum(IN_SPLITS)

kernel_name = "hybrid_dsa_shortconv_convffn_deepnorm_ple"


def _split_points():
    pts, acc = [], 0
    for w in IN_SPLITS[:-1]:
        acc += w
        pts.append(acc)
    return pts


def layer_norm(x, g, b):
    xf = x.astype(jnp.float32)
    mu = jnp.mean(xf, axis=-1, keepdims=True)
    var = jnp.mean(jnp.square(xf - mu), axis=-1, keepdims=True)
    y = (xf - mu) * lax.rsqrt(var + LN_EPS) * g.astype(jnp.float32) + b.astype(jnp.float32)
    return y.astype(x.dtype)


def rms_norm(x, g):
    xf = x.astype(jnp.float32)
    y = xf * lax.rsqrt(jnp.mean(jnp.square(xf), axis=-1, keepdims=True) + LN_EPS) * g.astype(jnp.float32)
    return y.astype(x.dtype)


def causal_dwconv(u, w, b):
    c = u.shape[-1]
    k = w.shape[0]
    out = lax.conv_general_dilated(
        u, w[:, None, :].astype(u.dtype), window_strides=(1,), padding=[(k - 1, 0)],
        dimension_numbers=('NWC', 'WIO', 'NWC'), feature_group_count=c)
    return out + b.astype(u.dtype)


def dsa_sparse_attention(q_lat, c_kv, q_idx, k_idx, w_idx):
    bsz, seq = c_kv.shape[0], c_kv.shape[1]
    topk = min(TOPK_MAX, seq // 4)
    n_blocks = seq // Q_BLOCK
    key_pos = jnp.arange(seq)
    att_scale = ATT_HEAD_DIM ** -0.5
    idx_scale = IDX_HEAD_DIM ** -0.5

    def to_blocks(a):
        return jnp.swapaxes(a.reshape((bsz, n_blocks, Q_BLOCK) + a.shape[2:]), 0, 1)

    def one_block(args):
        blk, qb, qib, wb = args
        q_pos = blk * Q_BLOCK + jnp.arange(Q_BLOCK)
        causal = key_pos[None, :] <= q_pos[:, None]
        raw = jnp.einsum('bqhd,bsd->bqhs', qib, k_idx).astype(jnp.float32) * idx_scale
        score = jnp.einsum('bqhs,bqh->bqs', jax.nn.relu(raw), wb.astype(jnp.float32))
        score = jnp.where(causal[None], score, -jnp.inf)
        _, sel = lax.top_k(score, topk)
        valid = sel <= q_pos[None, :, None]
        c_sel = jax.vmap(lambda c, i: c[i])(c_kv, sel)
        logits = jnp.einsum('bqhr,bqkr->bqhk', qb, c_sel).astype(jnp.float32) * att_scale
        logits = jnp.where(valid[:, :, None, :], logits, -jnp.inf)
        probs = jax.nn.softmax(logits, axis=-1).astype(c_sel.dtype)
        return jnp.einsum('bqhk,bqkr->bqhr', probs, c_sel)

    out = lax.map(one_block, (jnp.arange(n_blocks), to_blocks(q_lat), to_blocks(q_idx), to_blocks(w_idx)))
    return jnp.swapaxes(out, 0, 1).reshape((bsz, seq) + q_lat.shape[2:])


def setup_inputs(seed: int = 0) -> dict:
    key = jax.random.key(seed)
    ks = iter(jax.random.split(key, 40))
    f32 = jnp.float32

    def nrm(shape, scale):
        return jax.random.normal(next(ks), shape, f32) * scale

    def gain(shape):
        return 1.0 + 0.02 * jax.random.normal(next(ks), shape, f32)

    def bias(shape):
        return 0.02 * jax.random.normal(next(ks), shape, f32)

    L = DEPTH
    beta = DEEPNORM_BETA
    return {
        "x": jax.random.normal(next(ks), (BATCH, SEQ, D_MODEL), f32),
        "p": jax.random.normal(next(ks), (DEPTH, BATCH, SEQ, PLE_DIM), f32),
        "ln_emb_g": gain((D_MODEL,)),
        "ln_emb_b": bias((D_MODEL,)),
        "w_in": nrm((L, D_MODEL, IN_WIDTH), D_MODEL ** -0.5),
        "b_gate": bias((L, N_BRANCHES, D_MODEL)),
        "kv_norm_g": gain((L, KV_RANK)),
        "w_uk": nrm((L, N_ATT_HEADS, KV_RANK, ATT_HEAD_DIM), KV_RANK ** -0.5),
        "w_uv": nrm((L, N_ATT_HEADS, KV_RANK, ATT_HEAD_DIM), KV_RANK ** -0.5 * beta),
        "k_idx_ln_g": gain((L, IDX_HEAD_DIM)),
        "k_idx_ln_b": bias((L, IDX_HEAD_DIM)),
        "mix_conv_w": nrm((L, CONV_K, D_CONV), CONV_K ** -0.5),
        "mix_conv_b": bias((L, D_CONV)),
        "w_br_att": nrm((L, ATT_WIDTH, D_MODEL), ATT_WIDTH ** -0.5 * beta),
        "w_br_conv": nrm((L, D_CONV, D_MODEL), D_CONV ** -0.5 * beta),
        "w_o": nrm((L, D_MODEL, D_MODEL), D_MODEL ** -0.5 * beta),
        "ln1_g": gain((L, D_MODEL)),
        "ln1_b": bias((L, D_MODEL)),
        "w_ffn_up": nrm((L, D_MODEL, 2 * D_FF), D_MODEL ** -0.5),
        "ffn_conv_w": nrm((L, FFN_CONV_K, 2 * D_FF), FFN_CONV_K ** -0.5),
        "ffn_conv_b": bias((L, 2 * D_FF)),
        "w_ffn_down": nrm((L, D_FF, D_MODEL), D_FF ** -0.5 * beta),
        "w_ple_gate": nrm((L, D_MODEL, D_MODEL), D_MODEL ** -0.5),
        "b_ple_gate": bias((L, D_MODEL)),
        "w_ple": nrm((L, PLE_DIM, D_MODEL), PLE_DIM ** -0.5 * beta),
        "ln2_g": gain((L, D_MODEL)),
        "ln2_b": bias((L, D_MODEL)),
    }


def reference(x, p, ln_emb_g, ln_emb_b, w_in, b_gate, kv_norm_g, w_uk, w_uv, k_idx_ln_g, k_idx_ln_b,
              mix_conv_w, mix_conv_b, w_br_att, w_br_conv, w_o, ln1_g, ln1_b, w_ffn_up, ffn_conv_w,
              ffn_conv_b, w_ffn_down, w_ple_gate, b_ple_gate, w_ple, ln2_g, ln2_b):
    bsz, seq = x.shape[0], x.shape[1]
    split_points = _split_points()
    h = layer_norm(x, ln_emb_g, ln_emb_b)
    for i in range(DEPTH):
        proj = h @ w_in[i]
        (q_att, c_kv, q_idx, k_idx, w_idx, cv_b, cv_c, cv_x, g_att, g_conv) = jnp.split(proj, split_points, axis=-1)

        q_att = q_att.reshape(bsz, seq, N_ATT_HEADS, ATT_HEAD_DIM)
        c_kv = rms_norm(c_kv, kv_norm_g[i])
        q_lat = jnp.einsum('bshd,hrd->bshr', q_att, w_uk[i])
        q_idx = q_idx.reshape(bsz, seq, N_IDX_HEADS, IDX_HEAD_DIM)
        k_idx = layer_norm(k_idx, k_idx_ln_g[i], k_idx_ln_b[i])
        w_idx = w_idx * (N_IDX_HEADS ** -0.5)
        o_lat = dsa_sparse_attention(q_lat, c_kv, q_idx, k_idx, w_idx)
        att = jnp.einsum('bshr,hrd->bshd', o_lat, w_uv[i]).reshape(bsz, seq, ATT_WIDTH)

        conv_y = cv_b * causal_dwconv(cv_c * cv_x, mix_conv_w[i], mix_conv_b[i])

        merged = (jax.nn.sigmoid(g_att + b_gate[i, 0]) * (att @ w_br_att[i])
                  + jax.nn.sigmoid(g_conv + b_gate[i, 1]) * (conv_y @ w_br_conv[i]))
        h = layer_norm(DEEPNORM_ALPHA * h + merged @ w_o[i], ln1_g[i], ln1_b[i])

        gu = causal_dwconv(h @ w_ffn_up[i], ffn_conv_w[i], ffn_conv_b[i])
        g_ffn, u_ffn = jnp.split(gu, 2, axis=-1)
        ffn = (jax.nn.silu(g_ffn) * u_ffn) @ w_ffn_down[i]
        ple = jax.nn.sigmoid(h @ w_ple_gate[i] + b_ple_gate[i]) * (p[i] @ w_ple[i])
        h = layer_norm(DEEPNORM_ALPHA * h + ffn + ple, ln2_g[i], ln2_b[i])
    return h
```

```python
import functools

import jax
import jax.numpy as jnp
from jax import lax
from jax.experimental import pallas as pl
from jax.experimental.pallas import tpu as pltpu

D_MODEL = 1024
PLE_DIM = 256
N_HEADS = 8
HEAD_DIM = 64
ATT_WIDTH = N_HEADS * HEAD_DIM
KV_RANK = 128
N_IDX_HEADS = 8
IDX_DIM = 64
TOPK_MAX = 256
Q_BLOCK = 128
D_CONV = 512
D_FF = 2816
LN_EPS = 1e-5
DEPTH = 1
ALPHA = (2.0 * DEPTH) ** 0.25
ATT_SCALE = HEAD_DIM ** -0.5
IDX_SCALE = IDX_DIM ** -0.5

LANES = 128
SUBLANES = 8
FF_CHUNK = 256
N_FF_CHUNKS = D_FF // FF_CHUNK
TM_PRE = 512
TM_POST = 256
CNT_ROWS = 64
VMEM_LIMIT = 56 * 1024 * 1024

INT_MIN = -(2 ** 31)
KEY_NEG_INF = -2139095041

_TOK_CKV = 0
_TOK_KIDX = 128
_TOK_CVB = 256
_TOK_CVC = _TOK_CVB + D_CONV
_TOK_CVX = _TOK_CVC + D_CONV
_TOK_GATT = _TOK_CVX + D_CONV
_TOK_GCONV = _TOK_GATT + D_MODEL
_TOK_WIDTH = _TOK_GCONV + D_MODEL


def _layer_norm(x, g, b):
    mu = jnp.mean(x, axis=-1, keepdims=True)
    d = x - mu
    var = jnp.mean(d * d, axis=-1, keepdims=True)
    return d * lax.rsqrt(var + LN_EPS) * g + b


def _dot(a, b):
    return jnp.dot(a, b, preferred_element_type=jnp.float32)


def _dot_nt(a, b):
    return lax.dot_general(a, b, (((1,), (1,)), ((), ())), preferred_element_type=jnp.float32)


def _pre_kernel(x_ref, lng_ref, lnb_ref, wtok_ref, wqt_ref, wwt_ref, kvg_ref, kig_ref, kib_ref,
                wuk_ref, cw_ref, wbrc_ref, bg_ref,
                ckv_ref, ckvt_ref, kidx_ref, qlatt_ref, qidxt_ref, wt_ref, mconv_ref, sig_ref,
                zs_ref):
    tm = x_ref.shape[1]
    nsub = tm // Q_BLOCK

    @pl.when(pl.program_id(1) == 0)
    def _():
        zs_ref[0:SUBLANES, :] = jnp.zeros((SUBLANES, D_CONV), jnp.float32)

    h = _layer_norm(x_ref[0], lng_ref[...], lnb_ref[...])
    hb = h.astype(jnp.bfloat16)
    proj = _dot(hb, wtok_ref[...])

    c = proj[:, _TOK_CKV:_TOK_CKV + KV_RANK]
    c = c * lax.rsqrt(jnp.mean(c * c, axis=-1, keepdims=True) + LN_EPS) * kvg_ref[...]
    ckv_ref[0] = c.astype(jnp.bfloat16)
    for j in range(nsub):
        ckvt_ref[0, j] = c[j * Q_BLOCK:(j + 1) * Q_BLOCK, :].T.astype(jnp.bfloat16)

    k = proj[:, _TOK_KIDX:_TOK_KIDX + LANES]
    valid = lax.broadcasted_iota(jnp.int32, k.shape, 1) < IDX_DIM
    mu = jnp.sum(k, axis=-1, keepdims=True) * (1.0 / IDX_DIM)
    d = jnp.where(valid, k - mu, 0.0)
    var = jnp.sum(d * d, axis=-1, keepdims=True) * (1.0 / IDX_DIM)
    kn = d * lax.rsqrt(var + LN_EPS) * kig_ref[...] + kib_ref[...]
    kidx_ref[0] = kn[:, :IDX_DIM].astype(jnp.bfloat16)

    z = proj[:, _TOK_CVC:_TOK_CVC + D_CONV] * proj[:, _TOK_CVX:_TOK_CVX + D_CONV]
    zs_ref[SUBLANES:SUBLANES + tm, :] = z
    conv = (cw_ref[0:1, :] * zs_ref[SUBLANES - 2:SUBLANES - 2 + tm, :]
            + cw_ref[1:2, :] * zs_ref[SUBLANES - 1:SUBLANES - 1 + tm, :]
            + cw_ref[2:3, :] * z + cw_ref[3:4, :])
    zs_ref[0:SUBLANES, :] = z[tm - SUBLANES:tm, :]
    conv_y = proj[:, _TOK_CVB:_TOK_CVB + D_CONV] * conv
    g_conv = jax.nn.sigmoid(proj[:, _TOK_GCONV:_TOK_GCONV + D_MODEL] + bg_ref[1:2, :])
    mconv_ref[0] = (g_conv * _dot(conv_y.astype(jnp.bfloat16), wbrc_ref[...])).astype(jnp.bfloat16)
    sig_ref[0] = jax.nn.sigmoid(proj[:, _TOK_GATT:_TOK_GATT + D_MODEL] + bg_ref[0:1, :]).astype(jnp.bfloat16)

    projt = _dot_nt(wqt_ref[...], hb)
    wt = _dot_nt(wwt_ref[...], hb)[0:N_IDX_HEADS, :] * (N_IDX_HEADS ** -0.5) * IDX_SCALE
    for hd in range(N_HEADS):
        qa = projt[hd * HEAD_DIM:(hd + 1) * HEAD_DIM, :].astype(jnp.bfloat16)
        ql = (_dot(wuk_ref[hd], qa) * ATT_SCALE).astype(jnp.bfloat16)
        qi = projt[ATT_WIDTH + hd * IDX_DIM:ATT_WIDTH + (hd + 1) * IDX_DIM, :].astype(jnp.bfloat16)
        for j in range(nsub):
            qlatt_ref[0, j, :, hd * LANES:(hd + 1) * LANES] = ql[:, j * Q_BLOCK:(j + 1) * Q_BLOCK]
            qidxt_ref[0, j, :, hd * LANES:(hd + 1) * LANES] = qi[:, j * Q_BLOCK:(j + 1) * Q_BLOCK]
    for j in range(nsub):
        wt_ref[0, j] = wt[:, j * Q_BLOCK:(j + 1) * Q_BLOCK]


def _attn_kernel(qlatt_ref, qidxt_ref, wt_ref, ckv_ref, ckvt_ref, kidx_ref, wuvt_ref,
                 att_ref,
                 keys_ref, bias_ref, ot_ref, m_ref, s_ref, *, topk, idx_bits):
    blk = pl.program_id(1)
    nk = blk + 1
    q0 = blk * Q_BLOCK
    row_iota = lax.broadcasted_iota(jnp.int32, (Q_BLOCK, LANES), 0)
    qpos = q0 + lax.broadcasted_iota(jnp.int32, (Q_BLOCK, LANES), 1)

    def score_chunk(c, carry):
        r0 = pl.multiple_of(c * Q_BLOCK, Q_BLOCK)
        kc = kidx_ref[0, pl.ds(r0, Q_BLOCK), :]
        acc = jnp.zeros((Q_BLOCK, LANES), jnp.float32)
        for hd in range(N_IDX_HEADS):
            raw = _dot(kc, qidxt_ref[0, 0, :, hd * LANES:(hd + 1) * LANES])
            acc = acc + jnp.maximum(raw, 0.0) * wt_ref[0, 0, hd:hd + 1, :]
        acc = jnp.where(r0 + row_iota <= qpos, acc, -jnp.inf)
        bits = lax.bitcast_convert_type(acc, jnp.int32)
        keys_ref[pl.ds(r0, Q_BLOCK), :] = bits ^ ((bits >> 31) & 0x7FFFFFFF)
        return carry

    lax.fori_loop(0, nk, score_chunk, 0)

    n_cnt = nk * (Q_BLOCK // CNT_ROWS)

    def count(pred):
        def body(j, acc):
            r0 = pl.multiple_of(j * CNT_ROWS, CNT_ROWS)
            idx = r0 + lax.broadcasted_iota(jnp.int32, (CNT_ROWS, LANES), 0)
            return acc + jnp.where(pred(keys_ref[pl.ds(r0, CNT_ROWS), :], idx), 1, 0)
        acc = lax.fori_loop(0, n_cnt, body, jnp.zeros((CNT_ROWS, LANES), jnp.int32))
        return jnp.sum(acc, axis=0, keepdims=True)

    def bit_body(i, carry):
        t, cnt_t = carry
        cand = t + lax.shift_left(jnp.int32(1), 31 - i)
        cnt = count(lambda k, idx: k >= cand)
        take = cnt >= topk
        return jnp.where(take, cand, t), jnp.where(take, cnt, cnt_t)

    t0 = jnp.full((1, LANES), INT_MIN, jnp.int32)
    cnt0 = jnp.zeros((1, LANES), jnp.int32) + nk * Q_BLOCK
    thr, cnt_thr = lax.fori_loop(0, 32, bit_body, (t0, cnt0))

    @pl.when(jnp.max(cnt_thr) > topk)
    def _():
        need = topk - count(lambda k, idx: k > thr)

        def jbit(i, jmax):
            cand = jmax + lax.shift_left(jnp.int32(1), idx_bits - 1 - i)
            f = count(lambda k, idx: jnp.where(idx < cand, k, INT_MIN) == thr)
            return jnp.where(f < need, cand, jmax)

        jmax = lax.fori_loop(0, idx_bits, jbit, jnp.zeros((1, LANES), jnp.int32))

        def drop(j, carry):
            r0 = pl.multiple_of(j * CNT_ROWS, CNT_ROWS)
            idx = r0 + lax.broadcasted_iota(jnp.int32, (CNT_ROWS, LANES), 0)
            k = keys_ref[pl.ds(r0, CNT_ROWS), :]
            excess = jnp.where(idx > jmax, k, INT_MIN) == thr
            keys_ref[pl.ds(r0, CNT_ROWS), :] = jnp.where(excess, INT_MIN, k)
            return carry

        lax.fori_loop(0, n_cnt, drop, 0)

    thr_sel = jnp.maximum(thr, KEY_NEG_INF + 1)

    def bias_chunk(j, carry):
        r0 = pl.multiple_of(j * CNT_ROWS, CNT_ROWS)
        k = keys_ref[pl.ds(r0, CNT_ROWS), :]
        bias_ref[pl.ds(r0, CNT_ROWS), :] = jnp.where(k >= thr_sel, 0.0, -jnp.inf)
        return carry

    lax.fori_loop(0, n_cnt, bias_chunk, 0)

    m_ref[...] = jnp.full(m_ref.shape, -jnp.inf, jnp.float32)
    s_ref[...] = jnp.zeros(s_ref.shape, jnp.float32)
    ot_ref[...] = jnp.zeros(ot_ref.shape, jnp.float32)

    def att_chunk(c, carry):
        r0 = pl.multiple_of(c * Q_BLOCK, Q_BLOCK)
        kv = ckv_ref[0, pl.ds(r0, Q_BLOCK), :]
        kvt = ckvt_ref[0, c]
        bias = bias_ref[pl.ds(r0, Q_BLOCK), :]
        for hd in range(N_HEADS):
            lanes = slice(hd * LANES, (hd + 1) * LANES)
            logit = _dot(kv, qlatt_ref[0, 0, :, lanes]) + bias
            m_old = m_ref[hd:hd + 1, :]
            m_new = jnp.maximum(m_old, jnp.max(logit, axis=0, keepdims=True))
            m_safe = jnp.where(m_new == -jnp.inf, 0.0, m_new)
            p = jnp.exp(logit - m_safe)
            corr = jnp.exp(m_old - m_safe)
            s_ref[hd:hd + 1, :] = s_ref[hd:hd + 1, :] * corr + jnp.sum(p, axis=0, keepdims=True)
            ot_ref[:, lanes] = ot_ref[:, lanes] * corr + _dot(kvt, p.astype(jnp.bfloat16))
            m_ref[hd:hd + 1, :] = m_new
        return carry

    lax.fori_loop(0, nk, att_chunk, 0)

    outs = []
    for hd in range(N_HEADS):
        lanes = slice(hd * LANES, (hd + 1) * LANES)
        o = (ot_ref[:, lanes] / s_ref[hd:hd + 1, :]).astype(jnp.bfloat16)
        outs.append(_dot(wuvt_ref[hd], o))
    att_t = jnp.concatenate(outs, axis=0)
    att_ref[0] = att_t.T.astype(jnp.bfloat16)


def _post_kernel(x_ref, att_ref, sig_ref, mconv_ref, p_ref, lng_ref, lnb_ref, wbra_ref, wo_ref,
                 ln1g_ref, ln1b_ref, wup_ref, cw_ref, wdown_ref, wpg_ref, bpg_ref, wple_ref,
                 ln2g_ref, ln2b_ref,
                 out_ref,
                 zs_ref, carry_ref, h1b_ref, acc_ref):
    tm = x_ref.shape[1]

    @pl.when(pl.program_id(1) == 0)
    def _():
        carry_ref[...] = jnp.zeros(carry_ref.shape, jnp.float32)

    h = _layer_norm(x_ref[0], lng_ref[...], lnb_ref[...])
    merged = (sig_ref[0].astype(jnp.float32) * _dot(att_ref[0], wbra_ref[...])
              + mconv_ref[0].astype(jnp.float32))
    y = _dot(merged.astype(jnp.bfloat16), wo_ref[...])
    h1 = _layer_norm(ALPHA * h + y, ln1g_ref[...], ln1b_ref[...])
    h1b_ref[...] = h1.astype(jnp.bfloat16)
    acc_ref[...] = jnp.zeros(acc_ref.shape, jnp.float32)

    def ff_chunk(j, carry):
        pre = _dot(h1b_ref[...], wup_ref[j])
        zs_ref[0:SUBLANES, :] = carry_ref[j]
        zs_ref[SUBLANES:SUBLANES + tm, :] = pre
        cw = cw_ref[j]
        conv = (cw[0:1, :] * zs_ref[SUBLANES - 2:SUBLANES - 2 + tm, :]
                + cw[1:2, :] * zs_ref[SUBLANES - 1:SUBLANES - 1 + tm, :]
                + cw[2:3, :] * pre + cw[3:4, :])
        carry_ref[j] = pre[tm - SUBLANES:tm, :]
        g = conv[:, :FF_CHUNK]
        u = conv[:, FF_CHUNK:]
        act = (g * jax.nn.sigmoid(g) * u).astype(jnp.bfloat16)
        acc_ref[...] += _dot(act, wdown_ref[j])
        return carry

    lax.fori_loop(0, N_FF_CHUNKS, ff_chunk, 0)

    h1b = h1b_ref[...]
    ple = (jax.nn.sigmoid(_dot(h1b, wpg_ref[...]) + bpg_ref[...])
           * _dot(p_ref[0, 0].astype(jnp.bfloat16), wple_ref[...]))
    out_ref[0] = _layer_norm(ALPHA * h1 + acc_ref[...] + ple, ln2g_ref[...], ln2b_ref[...])


def _const_spec(shape):
    nd = len(shape)
    return pl.BlockSpec(shape, lambda *_: (0,) * nd, pipeline_mode=pl.Buffered(1))


def kernel(x, p, ln_emb_g, ln_emb_b, w_in, b_gate, kv_norm_g, w_uk, w_uv, k_idx_ln_g, k_idx_ln_b,
           mix_conv_w, mix_conv_b, w_br_att, w_br_conv, w_o, ln1_g, ln1_b, w_ffn_up, ffn_conv_w,
           ffn_conv_b, w_ffn_down, w_ple_gate, b_ple_gate, w_ple, ln2_g, ln2_b):
    bsz, seq, _ = x.shape
    assert seq % TM_PRE == 0 and seq % TM_POST == 0 and seq % Q_BLOCK == 0
    nb = seq // Q_BLOCK
    topk = min(TOPK_MAX, seq // 4)
    idx_bits = max(1, (seq - 1).bit_length())
    bf16, f32 = jnp.bfloat16, jnp.float32
    i = 0

    o_qatt = 0
    o_ckv = o_qatt + ATT_WIDTH
    o_qidx = o_ckv + KV_RANK
    o_kidx = o_qidx + N_IDX_HEADS * IDX_DIM
    o_widx = o_kidx + IDX_DIM
    o_cvb = o_widx + N_IDX_HEADS
    win = w_in[i]
    pad64 = jnp.zeros((D_MODEL, LANES - IDX_DIM), f32)
    w_tok = jnp.concatenate(
        [win[:, o_ckv:o_ckv + KV_RANK], win[:, o_kidx:o_kidx + IDX_DIM], pad64, win[:, o_cvb:]],
        axis=1).astype(bf16)
    w_qt = jnp.concatenate([win[:, o_qatt:o_qatt + ATT_WIDTH],
                            win[:, o_qidx:o_qidx + N_IDX_HEADS * IDX_DIM]], axis=1).T.astype(bf16)
    w_wt = jnp.concatenate([win[:, o_widx:o_widx + N_IDX_HEADS].T,
                            jnp.zeros((16 - N_IDX_HEADS, D_MODEL), f32)], axis=0).astype(bf16)
    row = lambda v: v.reshape(1, -1).astype(f32)
    kig = jnp.concatenate([k_idx_ln_g[i], jnp.zeros((LANES - IDX_DIM,), f32)]).reshape(1, LANES)
    kib = jnp.concatenate([k_idx_ln_b[i], jnp.zeros((LANES - IDX_DIM,), f32)]).reshape(1, LANES)
    cw_mix = jnp.concatenate([mix_conv_w[i], mix_conv_b[i][None, :],
                              jnp.zeros((SUBLANES - 4, D_CONV), f32)], axis=0)

    n_pre = seq // TM_PRE
    sub_pre = TM_PRE // Q_BLOCK
    tok3 = lambda w: pl.BlockSpec((1, TM_PRE, w), lambda b, t: (b, t, 0))
    blk4 = lambda r, w: pl.BlockSpec((1, sub_pre, r, w), lambda b, t: (b, t, 0, 0))
    pre_out_shapes = (
        jax.ShapeDtypeStruct((bsz, seq, KV_RANK), bf16),
        jax.ShapeDtypeStruct((bsz, nb, KV_RANK, Q_BLOCK), bf16),
        jax.ShapeDtypeStruct((bsz, seq, IDX_DIM), bf16),
        jax.ShapeDtypeStruct((bsz, nb, KV_RANK, N_HEADS * LANES), bf16),
        jax.ShapeDtypeStruct((bsz, nb, IDX_DIM, N_IDX_HEADS * LANES), bf16),
        jax.ShapeDtypeStruct((bsz, nb, N_IDX_HEADS, Q_BLOCK), f32),
        jax.ShapeDtypeStruct((bsz, seq, D_MODEL), bf16),
        jax.ShapeDtypeStruct((bsz, seq, D_MODEL), bf16),
    )
    ckv, ckvt, kidx, qlatt, qidxt, wt, mconv, sig = pl.pallas_call(
        _pre_kernel,
        grid=(bsz, n_pre),
        in_specs=[
            tok3(D_MODEL),
            _const_spec((1, D_MODEL)), _const_spec((1, D_MODEL)),
            _const_spec((D_MODEL, _TOK_WIDTH)), _const_spec((2 * ATT_WIDTH, D_MODEL)),
            _const_spec((16, D_MODEL)),
            _const_spec((1, KV_RANK)), _const_spec((1, LANES)), _const_spec((1, LANES)),
            _const_spec((N_HEADS, KV_RANK, HEAD_DIM)),
            _const_spec((SUBLANES, D_CONV)), _const_spec((D_CONV, D_MODEL)), _const_spec((2, D_MODEL)),
        ],
        out_specs=(
            tok3(KV_RANK), blk4(KV_RANK, Q_BLOCK), tok3(IDX_DIM),
            blk4(KV_RANK, N_HEADS * LANES), blk4(IDX_DIM, N_IDX_HEADS * LANES),
            blk4(N_IDX_HEADS, Q_BLOCK), tok3(D_MODEL), tok3(D_MODEL),
        ),
        out_shape=pre_out_shapes,
        scratch_shapes=[pltpu.VMEM((TM_PRE + SUBLANES, D_CONV), f32)],
        compiler_params=pltpu.CompilerParams(
            dimension_semantics=("arbitrary", "arbitrary"), vmem_limit_bytes=VMEM_LIMIT),
    )(x, row(ln_emb_g), row(ln_emb_b), w_tok, w_qt, w_wt, row(kv_norm_g[i]), kig, kib,
      w_uk[i].astype(bf16), cw_mix, w_br_conv[i].astype(bf16), b_gate[i].astype(f32))

    w_uvt = jnp.swapaxes(w_uv[i], 1, 2).astype(bf16)
    att = pl.pallas_call(
        functools.partial(_attn_kernel, topk=topk, idx_bits=idx_bits),
        grid=(bsz, nb),
        in_specs=[
            pl.BlockSpec((1, 1, KV_RANK, N_HEADS * LANES), lambda b, q: (b, q, 0, 0)),
            pl.BlockSpec((1, 1, IDX_DIM, N_IDX_HEADS * LANES), lambda b, q: (b, q, 0, 0)),
            pl.BlockSpec((1, 1, N_IDX_HEADS, Q_BLOCK), lambda b, q: (b, q, 0, 0)),
            pl.BlockSpec((1, seq, KV_RANK), lambda b, q: (b, 0, 0)),
            pl.BlockSpec((1, nb, KV_RANK, Q_BLOCK), lambda b, q: (b, 0, 0, 0)),
            pl.BlockSpec((1, seq, IDX_DIM), lambda b, q: (b, 0, 0)),
            _const_spec((N_HEADS, HEAD_DIM, KV_RANK)),
        ],
        out_specs=pl.BlockSpec((1, Q_BLOCK, ATT_WIDTH), lambda b, q: (b, q, 0)),
        out_shape=jax.ShapeDtypeStruct((bsz, seq, ATT_WIDTH), bf16),
        scratch_shapes=[
            pltpu.VMEM((seq, LANES), jnp.int32),
            pltpu.VMEM((seq, LANES), f32),
            pltpu.VMEM((KV_RANK, N_HEADS * LANES), f32),
            pltpu.VMEM((N_HEADS, LANES), f32),
            pltpu.VMEM((N_HEADS, LANES), f32),
        ],
        compiler_params=pltpu.CompilerParams(
            dimension_semantics=("arbitrary", "arbitrary"), vmem_limit_bytes=VMEM_LIMIT),
    )(qlatt, qidxt, wt, ckv, ckvt, kidx, w_uvt)

    wup = w_ffn_up[i]
    wup_c = jnp.concatenate(
        [wup[:, :D_FF].reshape(D_MODEL, N_FF_CHUNKS, FF_CHUNK),
         wup[:, D_FF:].reshape(D_MODEL, N_FF_CHUNKS, FF_CHUNK)], axis=2)
    wup_c = jnp.transpose(wup_c, (1, 0, 2)).astype(bf16)
    cwf = jnp.concatenate([ffn_conv_w[i], ffn_conv_b[i][None, :],
                           jnp.zeros((SUBLANES - 4, 2 * D_FF), f32)], axis=0)
    cwf_c = jnp.concatenate(
        [cwf[:, :D_FF].reshape(SUBLANES, N_FF_CHUNKS, FF_CHUNK),
         cwf[:, D_FF:].reshape(SUBLANES, N_FF_CHUNKS, FF_CHUNK)], axis=2)
    cwf_c = jnp.transpose(cwf_c, (1, 0, 2))
    wdown_c = w_ffn_down[i].reshape(N_FF_CHUNKS, FF_CHUNK, D_MODEL).astype(bf16)

    n_post = seq // TM_POST
    tokp = lambda w: pl.BlockSpec((1, TM_POST, w), lambda b, t: (b, t, 0))
    out = pl.pallas_call(
        _post_kernel,
        grid=(bsz, n_post),
        in_specs=[
            tokp(D_MODEL), tokp(ATT_WIDTH), tokp(D_MODEL), tokp(D_MODEL),
            pl.BlockSpec((1, 1, TM_POST, PLE_DIM), lambda b, t: (0, b, t, 0)),
            _const_spec((1, D_MODEL)), _const_spec((1, D_MODEL)),
            _const_spec((ATT_WIDTH, D_MODEL)), _const_spec((D_MODEL, D_MODEL)),
            _const_spec((1, D_MODEL)), _const_spec((1, D_MODEL)),
            _const_spec((N_FF_CHUNKS, D_MODEL, 2 * FF_CHUNK)),
            _const_spec((N_FF_CHUNKS, SUBLANES, 2 * FF_CHUNK)),
            _const_spec((N_FF_CHUNKS, FF_CHUNK, D_MODEL)),
            _const_spec((D_MODEL, D_MODEL)), _const_spec((1, D_MODEL)),
            _const_spec((PLE_DIM, D_MODEL)),
            _const_spec((1, D_MODEL)), _const_spec((1, D_MODEL)),
        ],
        out_specs=tokp(D_MODEL),
        out_shape=jax.ShapeDtypeStruct((bsz, seq, D_MODEL), x.dtype),
        scratch_shapes=[
            pltpu.VMEM((TM_POST + SUBLANES, 2 * FF_CHUNK), f32),
            pltpu.VMEM((N_FF_CHUNKS, SUBLANES, 2 * FF_CHUNK), f32),
            pltpu.VMEM((TM_POST, D_MODEL), bf16),
            pltpu.VMEM((TM_POST, D_MODEL), f32),
        ],
        compiler_params=pltpu.CompilerParams(
            dimension_semantics=("arbitrary", "arbitrary"), vmem_limit_bytes=VMEM_LIMIT),
    )(x, att, sig, mconv, p, row(ln_emb_g), row(ln_emb_b), w_br_att[i].astype(bf16),
      w_o[i].astype(bf16), row(ln1_g[i]), row(ln1_b[i]), wup_c, cwf_c, wdown_c,
      w_ple_gate[i].astype(bf16), row(b_ple_gate[i]), w_ple[i].astype(bf16),
      row(ln2_g[i]), row(ln2_b[i]))
    return out
```

```python
import functools

import jax
import jax.numpy as jnp
from jax import lax
from jax.experimental import pallas as pl
from jax.experimental.pallas import tpu as pltpu

D_MODEL = 1024
PLE_DIM = 256
N_HEADS = 8
HEAD_DIM = 64
ATT_WIDTH = N_HEADS * HEAD_DIM
KV_RANK = 128
N_IDX_HEADS = 8
IDX_DIM = 64
TOPK_MAX = 256
Q_BLOCK = 128
D_CONV = 512
D_FF = 2816
LN_EPS = 1e-5
DEPTH = 1
ALPHA = (2.0 * DEPTH) ** 0.25
ATT_SCALE = HEAD_DIM ** -0.5
IDX_SCALE = IDX_DIM ** -0.5

LANES = 128
SUBLANES = 8
FF_CHUNK = 256
N_FF_CHUNKS = D_FF // FF_CHUNK
TM_PRE = 512
TM_POST = 256
ATT_CHUNK = 256
VMEM_LIMIT = 56 * 1024 * 1024

INT_MIN = -(2 ** 31)
INT16_MIN = -(2 ** 15)
HI_MASK = -(2 ** 16)
SIGN16_X2 = -2147450880
LOG2E = 1.4426950408889634
OT_ROWS = KV_RANK + 16
KEY_NEG_INF = -2139095041

_TOK_CKV = 0
_TOK_KIDX = 128
_TOK_CVB = 256
_TOK_CVC = _TOK_CVB + D_CONV
_TOK_CVX = _TOK_CVC + D_CONV
_TOK_GATT = _TOK_CVX + D_CONV
_TOK_GCONV = _TOK_GATT + D_MODEL
_TOK_WIDTH = _TOK_GCONV + D_MODEL


def _layer_norm(x, g, b):
    mu = jnp.mean(x, axis=-1, keepdims=True)
    d = x - mu
    var = jnp.mean(d * d, axis=-1, keepdims=True)
    return d * lax.rsqrt(var + LN_EPS) * g + b


def _dot(a, b):
    return jnp.dot(a, b, preferred_element_type=jnp.float32)


def _dot_nt(a, b):
    return lax.dot_general(a, b, (((1,), (1,)), ((), ())), preferred_element_type=jnp.float32)


def _pre_kernel(x_ref, lng_ref, lnb_ref, wtok_ref, wqt_ref, wwt_ref, kvg_ref, kig_ref, kib_ref,
                wuk_ref, cw_ref, wbrc_ref, bg_ref,
                ckv_ref, ckvt_ref, kidx_ref, qlatt_ref, qidxt_ref, wt_ref, mconv_ref, sig_ref,
                zs_ref):
    tm = x_ref.shape[1]
    nsub = tm // Q_BLOCK

    @pl.when(pl.program_id(1) == 0)
    def _():
        zs_ref[0:SUBLANES, :] = jnp.zeros((SUBLANES, D_CONV), jnp.float32)

    h = _layer_norm(x_ref[0], lng_ref[...], lnb_ref[...])
    hb = h.astype(jnp.bfloat16)
    proj = _dot(hb, wtok_ref[...])

    c = proj[:, _TOK_CKV:_TOK_CKV + KV_RANK]
    c = c * lax.rsqrt(jnp.mean(c * c, axis=-1, keepdims=True) + LN_EPS) * kvg_ref[...]
    ckv_ref[0] = c.astype(jnp.bfloat16)
    ones_row = (lax.broadcasted_iota(jnp.int32, (OT_ROWS - KV_RANK, ATT_CHUNK), 0) == 0)
    for j in range(tm // ATT_CHUNK):
        ckvt_ref[0, j, 0:KV_RANK, :] = c[j * ATT_CHUNK:(j + 1) * ATT_CHUNK, :].T.astype(jnp.bfloat16)
        ckvt_ref[0, j, KV_RANK:OT_ROWS, :] = jnp.where(ones_row, 1.0, 0.0).astype(jnp.bfloat16)

    k = proj[:, _TOK_KIDX:_TOK_KIDX + LANES]
    valid = lax.broadcasted_iota(jnp.int32, k.shape, 1) < IDX_DIM
    mu = jnp.sum(k, axis=-1, keepdims=True) * (1.0 / IDX_DIM)
    d = jnp.where(valid, k - mu, 0.0)
    var = jnp.sum(d * d, axis=-1, keepdims=True) * (1.0 / IDX_DIM)
    kn = d * lax.rsqrt(var + LN_EPS) * kig_ref[...] + kib_ref[...]
    kidx_ref[0] = kn[:, :IDX_DIM].astype(jnp.bfloat16)

    z = proj[:, _TOK_CVC:_TOK_CVC + D_CONV] * proj[:, _TOK_CVX:_TOK_CVX + D_CONV]
    zs_ref[SUBLANES:SUBLANES + tm, :] = z
    conv = (cw_ref[0:1, :] * zs_ref[SUBLANES - 2:SUBLANES - 2 + tm, :]
            + cw_ref[1:2, :] * zs_ref[SUBLANES - 1:SUBLANES - 1 + tm, :]
            + cw_ref[2:3, :] * z + cw_ref[3:4, :])
    zs_ref[0:SUBLANES, :] = z[tm - SUBLANES:tm, :]
    conv_y = proj[:, _TOK_CVB:_TOK_CVB + D_CONV] * conv
    g_conv = jax.nn.sigmoid(proj[:, _TOK_GCONV:_TOK_GCONV + D_MODEL] + bg_ref[1:2, :])
    mconv_ref[0] = (g_conv * _dot(conv_y.astype(jnp.bfloat16), wbrc_ref[...])).astype(jnp.bfloat16)
    sig_ref[0] = jax.nn.sigmoid(proj[:, _TOK_GATT:_TOK_GATT + D_MODEL] + bg_ref[0:1, :]).astype(jnp.bfloat16)

    projt = _dot_nt(wqt_ref[...], hb)
    wt = _dot_nt(wwt_ref[...], hb)[0:N_IDX_HEADS, :] * (N_IDX_HEADS ** -0.5) * IDX_SCALE
    for hd in range(N_HEADS):
        qa = projt[hd * HEAD_DIM:(hd + 1) * HEAD_DIM, :].astype(jnp.bfloat16)
        ql = (_dot(wuk_ref[hd], qa) * (ATT_SCALE * LOG2E)).astype(jnp.bfloat16)
        qi = projt[ATT_WIDTH + hd * IDX_DIM:ATT_WIDTH + (hd + 1) * IDX_DIM, :].astype(jnp.bfloat16)
        for j in range(nsub):
            qlatt_ref[0, j, :, hd * LANES:(hd + 1) * LANES] = ql[:, j * Q_BLOCK:(j + 1) * Q_BLOCK]
            qidxt_ref[0, j, :, hd * LANES:(hd + 1) * LANES] = qi[:, j * Q_BLOCK:(j + 1) * Q_BLOCK]
    for j in range(nsub):
        wt_ref[0, j] = wt[:, j * Q_BLOCK:(j + 1) * Q_BLOCK]


def _attn_kernel(qlatt_ref, qidxt_ref, wt_ref, ckv_ref, ckvt_ref, kidx_ref, wuvt_ref,
                 att_ref,
                 keys_ref, hi_ref, lo_ref, bias_ref, rawa_ref, rawb_ref, lga_ref, lgb_ref, pa_ref, pb_ref,
                 corra_ref, corrb_ref, ot_ref, m_ref, *, topk, idx_bits):
    blk = pl.program_id(1)
    npair = (blk + 2) // 2
    nk2 = 2 * npair
    nquad = (npair + 1) // 2
    q0 = blk * Q_BLOCK
    row_iota = lax.broadcasted_iota(jnp.int32, (Q_BLOCK, LANES), 0)
    qpos = q0 + lax.broadcasted_iota(jnp.int32, (Q_BLOCK, LANES), 1)

    def raw_dot(c):
        r0 = pl.multiple_of(c * Q_BLOCK, Q_BLOCK)
        return _dot(kidx_ref[0, pl.ds(r0, Q_BLOCK), :], qidxt_ref[0, 0])

    def score_keys(raw_ref, c):
        acc = jnp.zeros((Q_BLOCK, LANES), jnp.float32)
        for hd in range(N_IDX_HEADS):
            acc = acc + (jnp.maximum(raw_ref[:, hd * LANES:(hd + 1) * LANES], 0.0)
                         * wt_ref[0, 0, hd:hd + 1, :])
        r0 = pl.multiple_of(c * Q_BLOCK, Q_BLOCK)
        acc = jnp.where(r0 + row_iota <= qpos, acc, -jnp.inf)
        bits = lax.bitcast_convert_type(acc, jnp.int32)
        key = bits ^ ((bits >> 31) & 0x7FFFFFFF)
        keys_ref[pl.ds(r0, Q_BLOCK), :] = key
        return key

    @pl.when(npair % 2 == 1)
    def _():
        r0 = pl.multiple_of(npair * Q_BLOCK, Q_BLOCK)
        hi_ref[pl.ds(r0, Q_BLOCK), :] = jnp.full((Q_BLOCK, LANES), SIGN16_X2, jnp.int32)
        lo_ref[pl.ds(r0, Q_BLOCK), :] = jnp.full((Q_BLOCK, LANES), SIGN16_X2, jnp.int32)
        r1 = pl.multiple_of(npair * ATT_CHUNK, ATT_CHUNK)
        bias_ref[pl.ds(r1, ATT_CHUNK), :] = jnp.full((ATT_CHUNK, LANES), -jnp.inf, jnp.float32)

    rawa_ref[...] = raw_dot(0)

    def score_pair(i, carry):
        rawb_ref[...] = raw_dot(2 * i + 1)
        ka = score_keys(rawa_ref, 2 * i)
        rawa_ref[...] = raw_dot(jnp.minimum(2 * i + 2, nk2 - 2))
        kb = score_keys(rawb_ref, 2 * i + 1)
        r0 = pl.multiple_of(i * Q_BLOCK, Q_BLOCK)
        hi_ref[pl.ds(r0, Q_BLOCK), :] = lax.shift_right_logical(ka, 16) | (kb & HI_MASK)
        lo_ref[pl.ds(r0, Q_BLOCK), :] = ((ka & 0xFFFF) | lax.shift_left(kb, 16)) ^ SIGN16_X2
        return carry

    lax.fori_loop(0, npair, score_pair, 0)

    def count16(ref, cand):
        c32 = (cand & 0xFFFF) | lax.shift_left(cand, 16)
        c16 = pltpu.bitcast(jnp.broadcast_to(c32, (SUBLANES, LANES)), jnp.int16)

        def body(j, acc):
            r0 = pl.multiple_of(j * ATT_CHUNK, ATT_CHUNK)
            k = pltpu.bitcast(ref[pl.ds(r0, ATT_CHUNK), :], jnp.int16)
            ones = jnp.where(k.reshape(32, 16, LANES) >= c16[None], jnp.int16(1), jnp.int16(0))
            parts = [ones[g] for g in range(32)]
            while len(parts) > 1:
                parts = [parts[g] + parts[g + 1] for g in range(0, len(parts), 2)]
            return acc + parts[0]

        acc = lax.fori_loop(0, nquad, body, jnp.zeros((16, LANES), jnp.int16))
        a32 = pltpu.bitcast(acc, jnp.int32)
        return jnp.sum((a32 & 0xFFFF) + lax.shift_right_logical(a32, 16), axis=0, keepdims=True)

    def search16(ref, need):
        def bit_body(i, carry):
            t, above = carry
            cand = t + lax.shift_left(jnp.int32(1), 15 - i)
            cnt = count16(ref, cand)
            take = cnt >= need
            return jnp.where(take, cand, t), jnp.where(take, above, cnt)
        return lax.fori_loop(0, 16, bit_body, (jnp.full((1, LANES), INT16_MIN, jnp.int32),
                                               jnp.zeros((1, LANES), jnp.int32)))

    t_hi, above_hi = search16(hi_ref, topk)

    t_hi32 = (t_hi & 0xFFFF) | lax.shift_left(t_hi, 16)
    t_hi16 = pltpu.bitcast(jnp.broadcast_to(t_hi32, (SUBLANES, LANES)), jnp.int16)

    def lo_prep(j, carry):
        r0 = pl.multiple_of(j * Q_BLOCK, Q_BLOCK)
        h16 = pltpu.bitcast(hi_ref[pl.ds(r0, Q_BLOCK), :], jnp.int16).reshape(16, 16, LANES)
        l16 = pltpu.bitcast(lo_ref[pl.ds(r0, Q_BLOCK), :], jnp.int16).reshape(16, 16, LANES)
        kept = jnp.where(h16 == t_hi16[None], l16, jnp.int16(INT16_MIN)).reshape(2 * Q_BLOCK, LANES)
        lo_ref[pl.ds(r0, Q_BLOCK), :] = pltpu.bitcast(kept, jnp.int32)
        return carry

    lax.fori_loop(0, 2 * nquad, lo_prep, 0)
    t_lo, _ = search16(lo_ref, topk - above_hi)
    thr = lax.shift_left(t_hi, 16) | (t_lo - INT16_MIN)
    thr = jnp.maximum(thr, KEY_NEG_INF + 1)

    def count32(pred):
        def body(j, acc):
            r0 = pl.multiple_of(j * Q_BLOCK, Q_BLOCK)
            idx = r0 + row_iota
            return acc + jnp.where(pred(keys_ref[pl.ds(r0, Q_BLOCK), :], idx), 1, 0)
        acc = lax.fori_loop(0, nk2, body, jnp.zeros((Q_BLOCK, LANES), jnp.int32))
        return jnp.sum(acc, axis=0, keepdims=True)

    def build_bias():
        def body(j, acc):
            r0 = pl.multiple_of(j * Q_BLOCK, Q_BLOCK)
            sel = keys_ref[pl.ds(r0, Q_BLOCK), :] >= thr
            bias_ref[pl.ds(r0, Q_BLOCK), :] = jnp.where(sel, 0.0, -jnp.inf)
            return acc + jnp.where(sel, 1, 0)
        acc = lax.fori_loop(0, nk2, body, jnp.zeros((Q_BLOCK, LANES), jnp.int32))
        return jnp.sum(acc, axis=0, keepdims=True)

    n_sel = build_bias()

    @pl.when(jnp.max(n_sel) > topk)
    def _():
        need = topk - count32(lambda k, idx: k > thr)

        def jbit(i, jmax):
            cand = jmax + lax.shift_left(jnp.int32(1), idx_bits - 1 - i)
            f = count32(lambda k, idx: jnp.where(k == thr, idx, cand) < cand)
            return jnp.where(f < need, cand, jmax)

        jmax = lax.fori_loop(0, idx_bits, jbit, jnp.zeros((1, LANES), jnp.int32))

        def drop(j, carry):
            r0 = pl.multiple_of(j * Q_BLOCK, Q_BLOCK)
            k = keys_ref[pl.ds(r0, Q_BLOCK), :]
            tie_idx = jnp.where(k == thr, r0 + row_iota, 0)
            bias_ref[pl.ds(r0, Q_BLOCK), :] = jnp.where(
                tie_idx > jmax, -jnp.inf, bias_ref[pl.ds(r0, Q_BLOCK), :])
            return carry

        lax.fori_loop(0, nk2, drop, 0)

    def qk_dot(c):
        r0 = pl.multiple_of(c * ATT_CHUNK, ATT_CHUNK)
        return _dot(ckv_ref[0, pl.ds(r0, ATT_CHUNK), :], qlatt_ref[0, 0])

    def pv_update(c, p_ref, corr_ref):
        pv = _dot(ckvt_ref[0, c], p_ref[...])
        for hd in range(N_HEADS):
            lanes = slice(hd * LANES, (hd + 1) * LANES)
            ot_ref[:, lanes] = ot_ref[:, lanes] * corr_ref[hd:hd + 1, :] + pv[:, lanes]

    def softmax_chunk(c, lg_ref, p_ref, corr_ref):
        r0 = pl.multiple_of(c * ATT_CHUNK, ATT_CHUNK)
        bias = bias_ref[pl.ds(r0, ATT_CHUNK), :]
        for hd in range(N_HEADS):
            lanes = slice(hd * LANES, (hd + 1) * LANES)
            logit = lg_ref[:, lanes] + bias
            m_old = m_ref[hd:hd + 1, :]
            m_new = jnp.maximum(m_old, jnp.max(logit, axis=0, keepdims=True))
            m_safe = jnp.where(m_new == -jnp.inf, 0.0, m_new)
            p_ref[:, lanes] = jnp.exp2(logit - m_safe).astype(jnp.bfloat16)
            corr_ref[hd:hd + 1, :] = jnp.exp2(m_old - m_safe)
            m_ref[hd:hd + 1, :] = m_new

    m_ref[...] = jnp.full(m_ref.shape, -jnp.inf, jnp.float32)
    ot_ref[...] = jnp.zeros(ot_ref.shape, jnp.float32)
    pb_ref[...] = jnp.zeros(pb_ref.shape, jnp.bfloat16)
    corrb_ref[...] = jnp.ones(corrb_ref.shape, jnp.float32)
    lga_ref[...] = qk_dot(0)
    last = 2 * nquad - 1

    def att_pair(i, carry):
        c0 = 2 * i
        lgb_ref[...] = qk_dot(c0 + 1)
        pv_update(jnp.maximum(c0 - 1, 0), pb_ref, corrb_ref)
        softmax_chunk(c0, lga_ref, pa_ref, corra_ref)
        lga_ref[...] = qk_dot(jnp.minimum(c0 + 2, last))
        pv_update(c0, pa_ref, corra_ref)
        softmax_chunk(c0 + 1, lgb_ref, pb_ref, corrb_ref)
        return carry

    lax.fori_loop(0, nquad, att_pair, 0)
    pv_update(last, pb_ref, corrb_ref)

    outs = []
    for hd in range(N_HEADS):
        lanes = slice(hd * LANES, (hd + 1) * LANES)
        o = (ot_ref[0:KV_RANK, lanes] / ot_ref[KV_RANK:KV_RANK + 1, lanes]).astype(jnp.bfloat16)
        outs.append(_dot(wuvt_ref[hd], o))
    att_t = jnp.concatenate(outs, axis=0)
    att_ref[0] = att_t.T.astype(jnp.bfloat16)


def _post_kernel(x_ref, att_ref, sig_ref, mconv_ref, p_ref, lng_ref, lnb_ref, wbra_ref, wo_ref,
                 ln1g_ref, ln1b_ref, wup_ref, cw_ref, wdown_ref, wpg_ref, bpg_ref, wple_ref,
                 ln2g_ref, ln2b_ref,
                 out_ref,
                 zs_ref, carry_ref, h1b_ref, acc_ref):
    tm = x_ref.shape[1]

    @pl.when(pl.program_id(1) == 0)
    def _():
        carry_ref[...] = jnp.zeros(carry_ref.shape, jnp.float32)

    h = _layer_norm(x_ref[0], lng_ref[...], lnb_ref[...])
    merged = (sig_ref[0].astype(jnp.float32) * _dot(att_ref[0], wbra_ref[...])
              + mconv_ref[0].astype(jnp.float32))
    y = _dot(merged.astype(jnp.bfloat16), wo_ref[...])
    h1 = _layer_norm(ALPHA * h + y, ln1g_ref[...], ln1b_ref[...])
    h1b_ref[...] = h1.astype(jnp.bfloat16)
    acc_ref[...] = jnp.zeros(acc_ref.shape, jnp.float32)

    def ff_chunk(j, carry):
        pre = _dot(h1b_ref[...], wup_ref[j])
        zs_ref[0:SUBLANES, :] = carry_ref[j]
        zs_ref[SUBLANES:SUBLANES + tm, :] = pre
        cw = cw_ref[j]
        conv = (cw[0:1, :] * zs_ref[SUBLANES - 2:SUBLANES - 2 + tm, :]
                + cw[1:2, :] * zs_ref[SUBLANES - 1:SUBLANES - 1 + tm, :]
                + cw[2:3, :] * pre + cw[3:4, :])
        carry_ref[j] = pre[tm - SUBLANES:tm, :]
        g = conv[:, :FF_CHUNK]
        u = conv[:, FF_CHUNK:]
        act = (g * jax.nn.sigmoid(g) * u).astype(jnp.bfloat16)
        acc_ref[...] += _dot(act, wdown_ref[j])
        return carry

    lax.fori_loop(0, N_FF_CHUNKS, ff_chunk, 0)

    h1b = h1b_ref[...]
    ple = (jax.nn.sigmoid(_dot(h1b, wpg_ref[...]) + bpg_ref[...])
           * _dot(p_ref[0, 0].astype(jnp.bfloat16), wple_ref[...]))
    out_ref[0] = _layer_norm(ALPHA * h1 + acc_ref[...] + ple, ln2g_ref[...], ln2b_ref[...])


def _const_spec(shape):
    nd = len(shape)
    return pl.BlockSpec(shape, lambda *_: (0,) * nd, pipeline_mode=pl.Buffered(1))


def kernel(x, p, ln_emb_g, ln_emb_b, w_in, b_gate, kv_norm_g, w_uk, w_uv, k_idx_ln_g, k_idx_ln_b,
           mix_conv_w, mix_conv_b, w_br_att, w_br_conv, w_o, ln1_g, ln1_b, w_ffn_up, ffn_conv_w,
           ffn_conv_b, w_ffn_down, w_ple_gate, b_ple_gate, w_ple, ln2_g, ln2_b):
    bsz, seq, _ = x.shape
    assert seq % TM_PRE == 0 and seq % TM_POST == 0 and TM_PRE % (2 * ATT_CHUNK) == 0
    nb = seq // Q_BLOCK
    topk = min(TOPK_MAX, seq // 4)
    idx_bits = max(1, (seq - 1).bit_length())
    bf16, f32 = jnp.bfloat16, jnp.float32
    i = 0

    o_qatt = 0
    o_ckv = o_qatt + ATT_WIDTH
    o_qidx = o_ckv + KV_RANK
    o_kidx = o_qidx + N_IDX_HEADS * IDX_DIM
    o_widx = o_kidx + IDX_DIM
    o_cvb = o_widx + N_IDX_HEADS
    win = w_in[i]
    pad64 = jnp.zeros((D_MODEL, LANES - IDX_DIM), f32)
    w_tok = jnp.concatenate(
        [win[:, o_ckv:o_ckv + KV_RANK], win[:, o_kidx:o_kidx + IDX_DIM], pad64, win[:, o_cvb:]],
        axis=1).astype(bf16)
    w_qt = jnp.concatenate([win[:, o_qatt:o_qatt + ATT_WIDTH],
                            win[:, o_qidx:o_qidx + N_IDX_HEADS * IDX_DIM]], axis=1).T.astype(bf16)
    w_wt = jnp.concatenate([win[:, o_widx:o_widx + N_IDX_HEADS].T,
                            jnp.zeros((16 - N_IDX_HEADS, D_MODEL), f32)], axis=0).astype(bf16)
    row = lambda v: v.reshape(1, -1).astype(f32)
    kig = jnp.concatenate([k_idx_ln_g[i], jnp.zeros((LANES - IDX_DIM,), f32)]).reshape(1, LANES)
    kib = jnp.concatenate([k_idx_ln_b[i], jnp.zeros((LANES - IDX_DIM,), f32)]).reshape(1, LANES)
    cw_mix = jnp.concatenate([mix_conv_w[i], mix_conv_b[i][None, :],
                              jnp.zeros((SUBLANES - 4, D_CONV), f32)], axis=0)

    n_pre = seq // TM_PRE
    sub_pre = TM_PRE // Q_BLOCK
    tok3 = lambda w: pl.BlockSpec((1, TM_PRE, w), lambda b, t: (b, t, 0))
    blk4 = lambda r, w: pl.BlockSpec((1, sub_pre, r, w), lambda b, t: (b, t, 0, 0))
    pre_out_shapes = (
        jax.ShapeDtypeStruct((bsz, seq, KV_RANK), bf16),
        jax.ShapeDtypeStruct((bsz, seq // ATT_CHUNK, OT_ROWS, ATT_CHUNK), bf16),
        jax.ShapeDtypeStruct((bsz, seq, IDX_DIM), bf16),
        jax.ShapeDtypeStruct((bsz, nb, KV_RANK, N_HEADS * LANES), bf16),
        jax.ShapeDtypeStruct((bsz, nb, IDX_DIM, N_IDX_HEADS * LANES), bf16),
        jax.ShapeDtypeStruct((bsz, nb, N_IDX_HEADS, Q_BLOCK), f32),
        jax.ShapeDtypeStruct((bsz, seq, D_MODEL), bf16),
        jax.ShapeDtypeStruct((bsz, seq, D_MODEL), bf16),
    )
    ckv, ckvt, kidx, qlatt, qidxt, wt, mconv, sig = pl.pallas_call(
        _pre_kernel,
        grid=(bsz, n_pre),
        in_specs=[
            tok3(D_MODEL),
            _const_spec((1, D_MODEL)), _const_spec((1, D_MODEL)),
            _const_spec((D_MODEL, _TOK_WIDTH)), _const_spec((2 * ATT_WIDTH, D_MODEL)),
            _const_spec((16, D_MODEL)),
            _const_spec((1, KV_RANK)), _const_spec((1, LANES)), _const_spec((1, LANES)),
            _const_spec((N_HEADS, KV_RANK, HEAD_DIM)),
            _const_spec((SUBLANES, D_CONV)), _const_spec((D_CONV, D_MODEL)), _const_spec((2, D_MODEL)),
        ],
        out_specs=(
            tok3(KV_RANK),
            pl.BlockSpec((1, TM_PRE // ATT_CHUNK, OT_ROWS, ATT_CHUNK), lambda b, t: (b, t, 0, 0)),
            tok3(IDX_DIM),
            blk4(KV_RANK, N_HEADS * LANES), blk4(IDX_DIM, N_IDX_HEADS * LANES),
            blk4(N_IDX_HEADS, Q_BLOCK), tok3(D_MODEL), tok3(D_MODEL),
        ),
        out_shape=pre_out_shapes,
        scratch_shapes=[pltpu.VMEM((TM_PRE + SUBLANES, D_CONV), f32)],
        compiler_params=pltpu.CompilerParams(
            dimension_semantics=("arbitrary", "arbitrary"), vmem_limit_bytes=VMEM_LIMIT),
    )(x, row(ln_emb_g), row(ln_emb_b), w_tok, w_qt, w_wt, row(kv_norm_g[i]), kig, kib,
      w_uk[i].astype(bf16), cw_mix, w_br_conv[i].astype(bf16), b_gate[i].astype(f32))

    w_uvt = jnp.swapaxes(w_uv[i], 1, 2).astype(bf16)
    att = pl.pallas_call(
        functools.partial(_attn_kernel, topk=topk, idx_bits=idx_bits),
        grid=(bsz, nb),
        in_specs=[
            pl.BlockSpec((1, 1, KV_RANK, N_HEADS * LANES), lambda b, q: (b, q, 0, 0)),
            pl.BlockSpec((1, 1, IDX_DIM, N_IDX_HEADS * LANES), lambda b, q: (b, q, 0, 0)),
            pl.BlockSpec((1, 1, N_IDX_HEADS, Q_BLOCK), lambda b, q: (b, q, 0, 0)),
            pl.BlockSpec((1, seq, KV_RANK), lambda b, q: (b, 0, 0)),
            pl.BlockSpec((1, seq // ATT_CHUNK, OT_ROWS, ATT_CHUNK), lambda b, q: (b, 0, 0, 0)),
            pl.BlockSpec((1, seq, IDX_DIM), lambda b, q: (b, 0, 0)),
            _const_spec((N_HEADS, HEAD_DIM, KV_RANK)),
        ],
        out_specs=pl.BlockSpec((1, Q_BLOCK, ATT_WIDTH), lambda b, q: (b, q, 0)),
        out_shape=jax.ShapeDtypeStruct((bsz, seq, ATT_WIDTH), bf16),
        scratch_shapes=[
            pltpu.VMEM((seq, LANES), jnp.int32),
            pltpu.VMEM((seq // 2, LANES), jnp.int32),
            pltpu.VMEM((seq // 2, LANES), jnp.int32),
            pltpu.VMEM((seq, LANES), f32),
            pltpu.VMEM((Q_BLOCK, N_IDX_HEADS * LANES), f32),
            pltpu.VMEM((Q_BLOCK, N_IDX_HEADS * LANES), f32),
            pltpu.VMEM((ATT_CHUNK, N_HEADS * LANES), f32),
            pltpu.VMEM((ATT_CHUNK, N_HEADS * LANES), f32),
            pltpu.VMEM((ATT_CHUNK, N_HEADS * LANES), bf16),
            pltpu.VMEM((ATT_CHUNK, N_HEADS * LANES), bf16),
            pltpu.VMEM((N_HEADS, LANES), f32),
            pltpu.VMEM((N_HEADS, LANES), f32),
            pltpu.VMEM((OT_ROWS, N_HEADS * LANES), f32),
            pltpu.VMEM((N_HEADS, LANES), f32),
        ],
        compiler_params=pltpu.CompilerParams(
            dimension_semantics=("arbitrary", "arbitrary"), vmem_limit_bytes=VMEM_LIMIT),
    )(qlatt, qidxt, wt, ckv, ckvt, kidx, w_uvt)

    wup = w_ffn_up[i]
    wup_c = jnp.concatenate(
        [wup[:, :D_FF].reshape(D_MODEL, N_FF_CHUNKS, FF_CHUNK),
         wup[:, D_FF:].reshape(D_MODEL, N_FF_CHUNKS, FF_CHUNK)], axis=2)
    wup_c = jnp.transpose(wup_c, (1, 0, 2)).astype(bf16)
    cwf = jnp.concatenate([ffn_conv_w[i], ffn_conv_b[i][None, :],
                           jnp.zeros((SUBLANES - 4, 2 * D_FF), f32)], axis=0)
    cwf_c = jnp.concatenate(
        [cwf[:, :D_FF].reshape(SUBLANES, N_FF_CHUNKS, FF_CHUNK),
         cwf[:, D_FF:].reshape(SUBLANES, N_FF_CHUNKS, FF_CHUNK)], axis=2)
    cwf_c = jnp.transpose(cwf_c, (1, 0, 2))
    wdown_c = w_ffn_down[i].reshape(N_FF_CHUNKS, FF_CHUNK, D_MODEL).astype(bf16)

    n_post = seq // TM_POST
    tokp = lambda w: pl.BlockSpec((1, TM_POST, w), lambda b, t: (b, t, 0))
    out = pl.pallas_call(
        _post_kernel,
        grid=(bsz, n_post),
        in_specs=[
            tokp(D_MODEL), tokp(ATT_WIDTH), tokp(D_MODEL), tokp(D_MODEL),
            pl.BlockSpec((1, 1, TM_POST, PLE_DIM), lambda b, t: (0, b, t, 0)),
            _const_spec((1, D_MODEL)), _const_spec((1, D_MODEL)),
            _const_spec((ATT_WIDTH, D_MODEL)), _const_spec((D_MODEL, D_MODEL)),
            _const_spec((1, D_MODEL)), _const_spec((1, D_MODEL)),
            _const_spec((N_FF_CHUNKS, D_MODEL, 2 * FF_CHUNK)),
            _const_spec((N_FF_CHUNKS, SUBLANES, 2 * FF_CHUNK)),
            _const_spec((N_FF_CHUNKS, FF_CHUNK, D_MODEL)),
            _const_spec((D_MODEL, D_MODEL)), _const_spec((1, D_MODEL)),
            _const_spec((PLE_DIM, D_MODEL)),
            _const_spec((1, D_MODEL)), _const_spec((1, D_MODEL)),
        ],
        out_specs=tokp(D_MODEL),
        out_shape=jax.ShapeDtypeStruct((bsz, seq, D_MODEL), x.dtype),
        scratch_shapes=[
            pltpu.VMEM((TM_POST + SUBLANES, 2 * FF_CHUNK), f32),
            pltpu.VMEM((N_FF_CHUNKS, SUBLANES, 2 * FF_CHUNK), f32),
            pltpu.VMEM((TM_POST, D_MODEL), bf16),
            pltpu.VMEM((TM_POST, D_MODEL), f32),
        ],
        compiler_params=pltpu.CompilerParams(
            dimension_semantics=("arbitrary", "arbitrary"), vmem_limit_bytes=VMEM_LIMIT),
    )(x, att, sig, mconv, p, row(ln_emb_g), row(ln_emb_b), w_br_att[i].astype(bf16),
      w_o[i].astype(bf16), row(ln1_g[i]), row(ln1_b[i]), wup_c, cwf_c, wdown_c,
      w_ple_gate[i].astype(bf16), row(b_ple_gate[i]), w_ple[i].astype(bf16),
      row(ln2_g[i]), row(ln2_b[i]))
    return out
```

```python
import functools

import jax
import jax.numpy as jnp
from jax import lax
from jax.experimental import pallas as pl
from jax.experimental.pallas import tpu as pltpu

D_MODEL = 1024
PLE_DIM = 256
N_HEADS = 8
HEAD_DIM = 64
ATT_WIDTH = N_HEADS * HEAD_DIM
KV_RANK = 128
N_IDX_HEADS = 8
IDX_DIM = 64
TOPK_MAX = 256
Q_BLOCK = 128
D_CONV = 512
D_FF = 2816
LN_EPS = 1e-5
DEPTH = 1
ALPHA = (2.0 * DEPTH) ** 0.25
ATT_SCALE = HEAD_DIM ** -0.5
IDX_SCALE = IDX_DIM ** -0.5

LANES = 128
SUBLANES = 8
FF_CHUNK = 256
N_FF_CHUNKS = D_FF // FF_CHUNK
TM_PRE = 512
TM_POST = 256
ATT_CHUNK = 256
N_CNT_ACC = 4
VMEM_LIMIT = 56 * 1024 * 1024

INT_MIN = -(2 ** 31)
INT16_MIN = -(2 ** 15)
HI_MASK = -(2 ** 16)
SIGN16_X2 = -2147450880
LOG2E = 1.4426950408889634
OT_ROWS = KV_RANK + 16
KEY_NEG_INF = -2139095041
MASK_NEG = -(2.0 ** 126)

_TOK_CKV = 0
_TOK_KIDX = 128
_TOK_CVB = 256
_TOK_CVC = _TOK_CVB + D_CONV
_TOK_CVX = _TOK_CVC + D_CONV
_TOK_GATT = _TOK_CVX + D_CONV
_TOK_GCONV = _TOK_GATT + D_MODEL
_TOK_WIDTH = _TOK_GCONV + D_MODEL


def _layer_norm(x, g, b):
    mu = jnp.mean(x, axis=-1, keepdims=True)
    d = x - mu
    var = jnp.mean(d * d, axis=-1, keepdims=True)
    return d * lax.rsqrt(var + LN_EPS) * g + b


def _dot(a, b):
    return jnp.dot(a, b, preferred_element_type=jnp.float32)


def _dot_nt(a, b):
    return lax.dot_general(a, b, (((1,), (1,)), ((), ())), preferred_element_type=jnp.float32)


def _pre_kernel(x_ref, lng_ref, lnb_ref, wtok_ref, wqt_ref, wwt_ref, kvg_ref, kig_ref, kib_ref,
                wuk_ref, cw_ref, wbrc_ref, bg_ref,
                ckv_ref, ckvt_ref, kidx_ref, qlatt_ref, qidxt_ref, wt_ref, mconv_ref, sig_ref,
                zs_ref):
    tm = x_ref.shape[1]
    nsub = tm // Q_BLOCK

    @pl.when(pl.program_id(1) == 0)
    def _():
        zs_ref[0:SUBLANES, :] = jnp.zeros((SUBLANES, D_CONV), jnp.float32)

    h = _layer_norm(x_ref[0], lng_ref[...], lnb_ref[...])
    hb = h.astype(jnp.bfloat16)
    proj = _dot(hb, wtok_ref[...])

    c = proj[:, _TOK_CKV:_TOK_CKV + KV_RANK]
    c = c * lax.rsqrt(jnp.mean(c * c, axis=-1, keepdims=True) + LN_EPS) * kvg_ref[...]
    ckv_ref[0] = c.astype(jnp.bfloat16)
    ones_row = (lax.broadcasted_iota(jnp.int32, (OT_ROWS - KV_RANK, ATT_CHUNK), 0) == 0)
    for j in range(tm // ATT_CHUNK):
        ckvt_ref[0, j, 0:KV_RANK, :] = c[j * ATT_CHUNK:(j + 1) * ATT_CHUNK, :].T.astype(jnp.bfloat16)
        ckvt_ref[0, j, KV_RANK:OT_ROWS, :] = jnp.where(ones_row, 1.0, 0.0).astype(jnp.bfloat16)

    k = proj[:, _TOK_KIDX:_TOK_KIDX + LANES]
    valid = lax.broadcasted_iota(jnp.int32, k.shape, 1) < IDX_DIM
    mu = jnp.sum(k, axis=-1, keepdims=True) * (1.0 / IDX_DIM)
    d = jnp.where(valid, k - mu, 0.0)
    var = jnp.sum(d * d, axis=-1, keepdims=True) * (1.0 / IDX_DIM)
    kn = d * lax.rsqrt(var + LN_EPS) * kig_ref[...] + kib_ref[...]
    kidx_ref[0] = kn[:, :IDX_DIM].astype(jnp.bfloat16)

    z = proj[:, _TOK_CVC:_TOK_CVC + D_CONV] * proj[:, _TOK_CVX:_TOK_CVX + D_CONV]
    zs_ref[SUBLANES:SUBLANES + tm, :] = z
    conv = (cw_ref[0:1, :] * zs_ref[SUBLANES - 2:SUBLANES - 2 + tm, :]
            + cw_ref[1:2, :] * zs_ref[SUBLANES - 1:SUBLANES - 1 + tm, :]
            + cw_ref[2:3, :] * z + cw_ref[3:4, :])
    zs_ref[0:SUBLANES, :] = z[tm - SUBLANES:tm, :]
    conv_y = proj[:, _TOK_CVB:_TOK_CVB + D_CONV] * conv
    g_conv = jax.nn.sigmoid(proj[:, _TOK_GCONV:_TOK_GCONV + D_MODEL] + bg_ref[1:2, :])
    mconv_ref[0] = (g_conv * _dot(conv_y.astype(jnp.bfloat16), wbrc_ref[...])).astype(jnp.bfloat16)
    sig_ref[0] = jax.nn.sigmoid(proj[:, _TOK_GATT:_TOK_GATT + D_MODEL] + bg_ref[0:1, :]).astype(jnp.bfloat16)

    projt = _dot_nt(wqt_ref[...], hb)
    wt = _dot_nt(wwt_ref[...], hb)[0:N_IDX_HEADS, :] * (N_IDX_HEADS ** -0.5) * IDX_SCALE
    for hd in range(N_HEADS):
        qa = projt[hd * HEAD_DIM:(hd + 1) * HEAD_DIM, :].astype(jnp.bfloat16)
        ql = (_dot(wuk_ref[hd], qa) * (ATT_SCALE * LOG2E)).astype(jnp.bfloat16)
        qi = projt[ATT_WIDTH + hd * IDX_DIM:ATT_WIDTH + (hd + 1) * IDX_DIM, :].astype(jnp.bfloat16)
        for j in range(nsub):
            qlatt_ref[0, j, :, hd * LANES:(hd + 1) * LANES] = ql[:, j * Q_BLOCK:(j + 1) * Q_BLOCK]
            qidxt_ref[0, j, :, hd * LANES:(hd + 1) * LANES] = qi[:, j * Q_BLOCK:(j + 1) * Q_BLOCK]
    for j in range(nsub):
        wt_ref[0, j] = wt[:, j * Q_BLOCK:(j + 1) * Q_BLOCK]


def _attn_kernel(qlatt_ref, qidxt_ref, wt_ref, ckv_ref, ckvt_ref, kidx_ref, wuvt_ref,
                 att_ref,
                 keys_ref, hi_ref, lo_ref, kvb_ref, qaug_ref, rawa_ref, rawb_ref, lga_ref, lgb_ref, pa_ref, pb_ref,
                 corra_ref, corrb_ref, ot_ref, m_ref, *, topk, idx_bits):
    blk = pl.program_id(1)
    npair = (blk + 2) // 2
    nk2 = 2 * npair
    nquad = (npair + 1) // 2
    q0 = blk * Q_BLOCK
    row_iota = lax.broadcasted_iota(jnp.int32, (Q_BLOCK, LANES), 0)
    qpos = q0 + lax.broadcasted_iota(jnp.int32, (Q_BLOCK, LANES), 1)

    def raw_dot(c):
        r0 = pl.multiple_of(c * Q_BLOCK, Q_BLOCK)
        return _dot(kidx_ref[0, pl.ds(r0, Q_BLOCK), :], qidxt_ref[0, 0])

    def score_keys(raw_ref, c):
        acc = jnp.zeros((Q_BLOCK, LANES), jnp.float32)
        for hd in range(N_IDX_HEADS):
            acc = acc + (jnp.maximum(raw_ref[:, hd * LANES:(hd + 1) * LANES], 0.0)
                         * wt_ref[0, 0, hd:hd + 1, :])
        r0 = pl.multiple_of(c * Q_BLOCK, Q_BLOCK)
        acc = jnp.where(r0 + row_iota <= qpos, acc, -jnp.inf)
        bits = lax.bitcast_convert_type(acc, jnp.int32)
        key = bits ^ ((bits >> 31) & 0x7FFFFFFF)
        keys_ref[pl.ds(r0, Q_BLOCK), :] = key
        return key

    @pl.when(blk == 0)
    def _():
        kvb_ref[:, 0:KV_RANK] = ckv_ref[0]
        ident = jnp.where(row_iota == lax.broadcasted_iota(jnp.int32, (Q_BLOCK, LANES), 1), 1.0, 0.0)
        for hd in range(N_HEADS):
            qaug_ref[KV_RANK:2 * KV_RANK, hd * LANES:(hd + 1) * LANES] = ident.astype(jnp.bfloat16)

    qaug_ref[0:KV_RANK, :] = qlatt_ref[0, 0]

    @pl.when(npair % 2 == 1)
    def _():
        r0 = pl.multiple_of(npair * Q_BLOCK, Q_BLOCK)
        hi_ref[pl.ds(r0, Q_BLOCK), :] = jnp.full((Q_BLOCK, LANES), SIGN16_X2, jnp.int32)
        lo_ref[pl.ds(r0, Q_BLOCK), :] = jnp.full((Q_BLOCK, LANES), SIGN16_X2, jnp.int32)
        r1 = pl.multiple_of(npair * ATT_CHUNK, ATT_CHUNK)
        kvb_ref[pl.ds(r1, ATT_CHUNK), KV_RANK:2 * KV_RANK] = jnp.full(
            (ATT_CHUNK, LANES), MASK_NEG, jnp.bfloat16)

    rawa_ref[...] = raw_dot(0)

    def score_pair(i, carry):
        rawb_ref[...] = raw_dot(2 * i + 1)
        ka = score_keys(rawa_ref, 2 * i)
        rawa_ref[...] = raw_dot(jnp.minimum(2 * i + 2, nk2 - 2))
        kb = score_keys(rawb_ref, 2 * i + 1)
        r0 = pl.multiple_of(i * Q_BLOCK, Q_BLOCK)
        hi_ref[pl.ds(r0, Q_BLOCK), :] = lax.shift_right_logical(ka, 16) | (kb & HI_MASK)
        lo_ref[pl.ds(r0, Q_BLOCK), :] = ((ka & 0xFFFF) | lax.shift_left(kb, 16)) ^ SIGN16_X2
        return carry

    lax.fori_loop(0, npair, score_pair, 0)

    def count16(ref, cand):
        c32 = (cand & 0xFFFF) | lax.shift_left(cand, 16)
        c16 = pltpu.bitcast(jnp.broadcast_to(c32, (SUBLANES, LANES)), jnp.int16)

        def body(j, accs):
            r0 = pl.multiple_of(j * ATT_CHUNK, ATT_CHUNK)
            k = pltpu.bitcast(ref[pl.ds(r0, ATT_CHUNK), :], jnp.int16)
            ones = jnp.where(k.reshape(32, 16, LANES) >= c16[None], jnp.int16(1), jnp.int16(0))
            accs = list(accs)
            for g in range(32):
                accs[g % N_CNT_ACC] = accs[g % N_CNT_ACC] + ones[g]
            return tuple(accs)

        zero = jnp.zeros((16, LANES), jnp.int16)
        accs = lax.fori_loop(0, nquad, body, (zero,) * N_CNT_ACC)
        acc = (accs[0] + accs[1]) + (accs[2] + accs[3])
        a32 = pltpu.bitcast(acc, jnp.int32)
        return jnp.sum((a32 & 0xFFFF) + lax.shift_right_logical(a32, 16), axis=0, keepdims=True)

    def search16(ref, need):
        def bit_body(i, carry):
            t, above = carry
            cand = t + lax.shift_left(jnp.int32(1), 15 - i)
            cnt = count16(ref, cand)
            take = cnt >= need
            return jnp.where(take, cand, t), jnp.where(take, above, cnt)
        return lax.fori_loop(0, 16, bit_body, (jnp.full((1, LANES), INT16_MIN, jnp.int32),
                                               jnp.zeros((1, LANES), jnp.int32)))

    t_hi, above_hi = search16(hi_ref, topk)

    t_hi32 = (t_hi & 0xFFFF) | lax.shift_left(t_hi, 16)
    t_hi16 = pltpu.bitcast(jnp.broadcast_to(t_hi32, (SUBLANES, LANES)), jnp.int16)

    def lo_prep(j, carry):
        r0 = pl.multiple_of(j * Q_BLOCK, Q_BLOCK)
        h16 = pltpu.bitcast(hi_ref[pl.ds(r0, Q_BLOCK), :], jnp.int16).reshape(16, 16, LANES)
        l16 = pltpu.bitcast(lo_ref[pl.ds(r0, Q_BLOCK), :], jnp.int16).reshape(16, 16, LANES)
        kept = jnp.where(h16 == t_hi16[None], l16, jnp.int16(INT16_MIN)).reshape(2 * Q_BLOCK, LANES)
        lo_ref[pl.ds(r0, Q_BLOCK), :] = pltpu.bitcast(kept, jnp.int32)
        return carry

    lax.fori_loop(0, 2 * nquad, lo_prep, 0)
    t_lo, _ = search16(lo_ref, topk - above_hi)
    thr = lax.shift_left(t_hi, 16) | (t_lo - INT16_MIN)
    thr = jnp.maximum(thr, KEY_NEG_INF + 1)

    def count32(pred):
        def body(j, acc):
            r0 = pl.multiple_of(j * Q_BLOCK, Q_BLOCK)
            idx = r0 + row_iota
            return acc + jnp.where(pred(keys_ref[pl.ds(r0, Q_BLOCK), :], idx), 1, 0)
        acc = lax.fori_loop(0, nk2, body, jnp.zeros((Q_BLOCK, LANES), jnp.int32))
        return jnp.sum(acc, axis=0, keepdims=True)

    def build_bias():
        def body(j, acc):
            r0 = pl.multiple_of(j * Q_BLOCK, Q_BLOCK)
            sel = keys_ref[pl.ds(r0, Q_BLOCK), :] >= thr
            kvb_ref[pl.ds(r0, Q_BLOCK), KV_RANK:2 * KV_RANK] = jnp.where(sel, 0.0, MASK_NEG).astype(jnp.bfloat16)
            return acc + jnp.where(sel, 1, 0)
        acc = lax.fori_loop(0, nk2, body, jnp.zeros((Q_BLOCK, LANES), jnp.int32))
        return jnp.sum(acc, axis=0, keepdims=True)

    n_sel = build_bias()

    @pl.when(jnp.max(n_sel) > topk)
    def _():
        need = topk - count32(lambda k, idx: k > thr)

        def jbit(i, jmax):
            cand = jmax + lax.shift_left(jnp.int32(1), idx_bits - 1 - i)
            f = count32(lambda k, idx: jnp.where(k == thr, idx, cand) < cand)
            return jnp.where(f < need, cand, jmax)

        jmax = lax.fori_loop(0, idx_bits, jbit, jnp.zeros((1, LANES), jnp.int32))

        def drop(j, carry):
            r0 = pl.multiple_of(j * Q_BLOCK, Q_BLOCK)
            k = keys_ref[pl.ds(r0, Q_BLOCK), :]
            tie_idx = jnp.where(k == thr, r0 + row_iota, 0)
            old = kvb_ref[pl.ds(r0, Q_BLOCK), KV_RANK:2 * KV_RANK].astype(jnp.float32)
            kvb_ref[pl.ds(r0, Q_BLOCK), KV_RANK:2 * KV_RANK] = jnp.where(
                tie_idx > jmax, MASK_NEG, old).astype(jnp.bfloat16)
            return carry

        lax.fori_loop(0, nk2, drop, 0)

    def qk_dot(c):
        r0 = pl.multiple_of(c * ATT_CHUNK, ATT_CHUNK)
        return _dot(kvb_ref[pl.ds(r0, ATT_CHUNK), :], qaug_ref[...])

    def pv_update(c, p_ref, corr_ref):
        pv = _dot(ckvt_ref[0, c], p_ref[...])
        for hd in range(N_HEADS):
            lanes = slice(hd * LANES, (hd + 1) * LANES)
            ot_ref[:, lanes] = ot_ref[:, lanes] * corr_ref[hd:hd + 1, :] + pv[:, lanes]

    def softmax_chunk(lg_ref, p_ref, corr_ref):
        for hd in range(N_HEADS):
            lanes = slice(hd * LANES, (hd + 1) * LANES)
            logit = lg_ref[:, lanes]
            m_old = m_ref[hd:hd + 1, :]
            m_new = jnp.maximum(m_old, jnp.max(logit, axis=0, keepdims=True))
            p_ref[:, lanes] = jnp.exp2(logit - m_new).astype(jnp.bfloat16)
            corr_ref[hd:hd + 1, :] = jnp.exp2(m_old - m_new)
            m_ref[hd:hd + 1, :] = m_new

    m_ref[...] = jnp.full(m_ref.shape, -jnp.inf, jnp.float32)
    ot_ref[...] = jnp.zeros(ot_ref.shape, jnp.float32)
    pb_ref[...] = jnp.zeros(pb_ref.shape, jnp.bfloat16)
    corrb_ref[...] = jnp.ones(corrb_ref.shape, jnp.float32)
    lga_ref[...] = qk_dot(0)
    last = 2 * nquad - 1

    def att_pair(i, carry):
        c0 = 2 * i
        lgb_ref[...] = qk_dot(c0 + 1)
        pv_update(jnp.maximum(c0 - 1, 0), pb_ref, corrb_ref)
        softmax_chunk(lga_ref, pa_ref, corra_ref)
        lga_ref[...] = qk_dot(jnp.minimum(c0 + 2, last))
        pv_update(c0, pa_ref, corra_ref)
        softmax_chunk(lgb_ref, pb_ref, corrb_ref)
        return carry

    lax.fori_loop(0, nquad, att_pair, 0)
    pv_update(last, pb_ref, corrb_ref)

    outs = []
    for hd in range(N_HEADS):
        lanes = slice(hd * LANES, (hd + 1) * LANES)
        o = (ot_ref[0:KV_RANK, lanes] / ot_ref[KV_RANK:KV_RANK + 1, lanes]).astype(jnp.bfloat16)
        outs.append(_dot(wuvt_ref[hd], o))
    att_t = jnp.concatenate(outs, axis=0)
    att_ref[0] = att_t.T.astype(jnp.bfloat16)


def _post_kernel(x_ref, att_ref, sig_ref, mconv_ref, p_ref, lng_ref, lnb_ref, wbra_ref, wo_ref,
                 ln1g_ref, ln1b_ref, wup_ref, cw_ref, wdown_ref, wpg_ref, bpg_ref, wple_ref,
                 ln2g_ref, ln2b_ref,
                 out_ref,
                 prea_ref, preb_ref, carry_ref, h1b_ref, act_ref, acc_ref):
    tm = x_ref.shape[1]
    n_pairs = N_FF_CHUNKS // 2

    @pl.when(pl.program_id(1) == 0)
    def _():
        carry_ref[...] = jnp.zeros(carry_ref.shape, jnp.float32)

    h = _layer_norm(x_ref[0], lng_ref[...], lnb_ref[...])
    merged = (sig_ref[0].astype(jnp.float32) * _dot(att_ref[0], wbra_ref[...])
              + mconv_ref[0].astype(jnp.float32))
    y = _dot(merged.astype(jnp.bfloat16), wo_ref[...])
    h1 = _layer_norm(ALPHA * h + y, ln1g_ref[...], ln1b_ref[...])
    h1b_ref[...] = h1.astype(jnp.bfloat16)

    def up_dot(j):
        return _dot(h1b_ref[...], wup_ref[j])

    def conv_act(pre_ref, j):
        pre_ref[0:SUBLANES, :] = carry_ref[j]
        cw = cw_ref[j]
        pre = pre_ref[SUBLANES:SUBLANES + tm, :]
        conv = (cw[0:1, :] * pre_ref[SUBLANES - 2:SUBLANES - 2 + tm, :]
                + cw[1:2, :] * pre_ref[SUBLANES - 1:SUBLANES - 1 + tm, :]
                + cw[2:3, :] * pre + cw[3:4, :])
        carry_ref[j] = pre[tm - SUBLANES:tm, :]
        g = conv[:, :FF_CHUNK]
        u = conv[:, FF_CHUNK:]
        return (g * jax.nn.sigmoid(g) * u).astype(jnp.bfloat16)

    prea_ref[SUBLANES:SUBLANES + tm, :] = up_dot(0)
    act_ref[...] = jnp.zeros(act_ref.shape, jnp.bfloat16)
    acc_ref[...] = jnp.zeros(acc_ref.shape, jnp.float32)

    def ff_pair(i, carry):
        j0 = 2 * i
        r0 = pl.multiple_of(jnp.maximum(i - 1, 0) * (2 * FF_CHUNK), 2 * FF_CHUNK)
        acc_ref[...] += _dot(act_ref[...], wdown_ref[pl.ds(r0, 2 * FF_CHUNK), :])
        preb_ref[SUBLANES:SUBLANES + tm, :] = up_dot(j0 + 1)
        act_ref[:, 0:FF_CHUNK] = conv_act(prea_ref, j0)
        prea_ref[SUBLANES:SUBLANES + tm, :] = up_dot(jnp.minimum(j0 + 2, N_FF_CHUNKS - 1))
        act_ref[:, FF_CHUNK:2 * FF_CHUNK] = conv_act(preb_ref, j0 + 1)
        return carry

    lax.fori_loop(0, n_pairs, ff_pair, 0)
    ffn = acc_ref[...] + _dot(act_ref[...], wdown_ref[(n_pairs - 1) * 2 * FF_CHUNK:n_pairs * 2 * FF_CHUNK, :])
    if N_FF_CHUNKS % 2:
        ffn = ffn + _dot(conv_act(prea_ref, N_FF_CHUNKS - 1), wdown_ref[n_pairs * 2 * FF_CHUNK:D_FF, :])

    h1b = h1b_ref[...]
    ple = (jax.nn.sigmoid(_dot(h1b, wpg_ref[...]) + bpg_ref[...])
           * _dot(p_ref[0, 0].astype(jnp.bfloat16), wple_ref[...]))
    out_ref[0] = _layer_norm(ALPHA * h1 + ffn + ple, ln2g_ref[...], ln2b_ref[...])


def _const_spec(shape):
    nd = len(shape)
    return pl.BlockSpec(shape, lambda *_: (0,) * nd, pipeline_mode=pl.Buffered(1))


def kernel(x, p, ln_emb_g, ln_emb_b, w_in, b_gate, kv_norm_g, w_uk, w_uv, k_idx_ln_g, k_idx_ln_b,
           mix_conv_w, mix_conv_b, w_br_att, w_br_conv, w_o, ln1_g, ln1_b, w_ffn_up, ffn_conv_w,
           ffn_conv_b, w_ffn_down, w_ple_gate, b_ple_gate, w_ple, ln2_g, ln2_b):
    bsz, seq, _ = x.shape
    assert seq % TM_PRE == 0 and seq % TM_POST == 0 and TM_PRE % (2 * ATT_CHUNK) == 0
    nb = seq // Q_BLOCK
    topk = min(TOPK_MAX, seq // 4)
    idx_bits = max(1, (seq - 1).bit_length())
    bf16, f32 = jnp.bfloat16, jnp.float32
    i = 0

    o_qatt = 0
    o_ckv = o_qatt + ATT_WIDTH
    o_qidx = o_ckv + KV_RANK
    o_kidx = o_qidx + N_IDX_HEADS * IDX_DIM
    o_widx = o_kidx + IDX_DIM
    o_cvb = o_widx + N_IDX_HEADS
    win = w_in[i]
    pad64 = jnp.zeros((D_MODEL, LANES - IDX_DIM), f32)
    w_tok = jnp.concatenate(
        [win[:, o_ckv:o_ckv + KV_RANK], win[:, o_kidx:o_kidx + IDX_DIM], pad64, win[:, o_cvb:]],
        axis=1).astype(bf16)
    w_qt = jnp.concatenate([win[:, o_qatt:o_qatt + ATT_WIDTH],
                            win[:, o_qidx:o_qidx + N_IDX_HEADS * IDX_DIM]], axis=1).T.astype(bf16)
    w_wt = jnp.concatenate([win[:, o_widx:o_widx + N_IDX_HEADS].T,
                            jnp.zeros((16 - N_IDX_HEADS, D_MODEL), f32)], axis=0).astype(bf16)
    row = lambda v: v.reshape(1, -1).astype(f32)
    kig = jnp.concatenate([k_idx_ln_g[i], jnp.zeros((LANES - IDX_DIM,), f32)]).reshape(1, LANES)
    kib = jnp.concatenate([k_idx_ln_b[i], jnp.zeros((LANES - IDX_DIM,), f32)]).reshape(1, LANES)
    cw_mix = jnp.concatenate([mix_conv_w[i], mix_conv_b[i][None, :],
                              jnp.zeros((SUBLANES - 4, D_CONV), f32)], axis=0)

    n_pre = seq // TM_PRE
    sub_pre = TM_PRE // Q_BLOCK
    tok3 = lambda w: pl.BlockSpec((1, TM_PRE, w), lambda b, t: (b, t, 0))
    blk4 = lambda r, w: pl.BlockSpec((1, sub_pre, r, w), lambda b, t: (b, t, 0, 0))
    pre_out_shapes = (
        jax.ShapeDtypeStruct((bsz, seq, KV_RANK), bf16),
        jax.ShapeDtypeStruct((bsz, seq // ATT_CHUNK, OT_ROWS, ATT_CHUNK), bf16),
        jax.ShapeDtypeStruct((bsz, seq, IDX_DIM), bf16),
        jax.ShapeDtypeStruct((bsz, nb, KV_RANK, N_HEADS * LANES), bf16),
        jax.ShapeDtypeStruct((bsz, nb, IDX_DIM, N_IDX_HEADS * LANES), bf16),
        jax.ShapeDtypeStruct((bsz, nb, N_IDX_HEADS, Q_BLOCK), f32),
        jax.ShapeDtypeStruct((bsz, seq, D_MODEL), bf16),
        jax.ShapeDtypeStruct((bsz, seq, D_MODEL), bf16),
    )
    ckv, ckvt, kidx, qlatt, qidxt, wt, mconv, sig = pl.pallas_call(
        _pre_kernel,
        grid=(bsz, n_pre),
        in_specs=[
            tok3(D_MODEL),
            _const_spec((1, D_MODEL)), _const_spec((1, D_MODEL)),
            _const_spec((D_MODEL, _TOK_WIDTH)), _const_spec((2 * ATT_WIDTH, D_MODEL)),
            _const_spec((16, D_MODEL)),
            _const_spec((1, KV_RANK)), _const_spec((1, LANES)), _const_spec((1, LANES)),
            _const_spec((N_HEADS, KV_RANK, HEAD_DIM)),
            _const_spec((SUBLANES, D_CONV)), _const_spec((D_CONV, D_MODEL)), _const_spec((2, D_MODEL)),
        ],
        out_specs=(
            tok3(KV_RANK),
            pl.BlockSpec((1, TM_PRE // ATT_CHUNK, OT_ROWS, ATT_CHUNK), lambda b, t: (b, t, 0, 0)),
            tok3(IDX_DIM),
            blk4(KV_RANK, N_HEADS * LANES), blk4(IDX_DIM, N_IDX_HEADS * LANES),
            blk4(N_IDX_HEADS, Q_BLOCK), tok3(D_MODEL), tok3(D_MODEL),
        ),
        out_shape=pre_out_shapes,
        scratch_shapes=[pltpu.VMEM((TM_PRE + SUBLANES, D_CONV), f32)],
        compiler_params=pltpu.CompilerParams(
            dimension_semantics=("arbitrary", "arbitrary"), vmem_limit_bytes=VMEM_LIMIT),
    )(x, row(ln_emb_g), row(ln_emb_b), w_tok, w_qt, w_wt, row(kv_norm_g[i]), kig, kib,
      w_uk[i].astype(bf16), cw_mix, w_br_conv[i].astype(bf16), b_gate[i].astype(f32))

    w_uvt = jnp.swapaxes(w_uv[i], 1, 2).astype(bf16)
    att = pl.pallas_call(
        functools.partial(_attn_kernel, topk=topk, idx_bits=idx_bits),
        grid=(bsz, nb),
        in_specs=[
            pl.BlockSpec((1, 1, KV_RANK, N_HEADS * LANES), lambda b, q: (b, q, 0, 0)),
            pl.BlockSpec((1, 1, IDX_DIM, N_IDX_HEADS * LANES), lambda b, q: (b, q, 0, 0)),
            pl.BlockSpec((1, 1, N_IDX_HEADS, Q_BLOCK), lambda b, q: (b, q, 0, 0)),
            pl.BlockSpec((1, seq, KV_RANK), lambda b, q: (b, 0, 0)),
            pl.BlockSpec((1, seq // ATT_CHUNK, OT_ROWS, ATT_CHUNK), lambda b, q: (b, 0, 0, 0)),
            pl.BlockSpec((1, seq, IDX_DIM), lambda b, q: (b, 0, 0)),
            _const_spec((N_HEADS, HEAD_DIM, KV_RANK)),
        ],
        out_specs=pl.BlockSpec((1, Q_BLOCK, ATT_WIDTH), lambda b, q: (b, q, 0)),
        out_shape=jax.ShapeDtypeStruct((bsz, seq, ATT_WIDTH), bf16),
        scratch_shapes=[
            pltpu.VMEM((seq, LANES), jnp.int32),
            pltpu.VMEM((seq // 2, LANES), jnp.int32),
            pltpu.VMEM((seq // 2, LANES), jnp.int32),
            pltpu.VMEM((seq, 2 * KV_RANK), bf16),
            pltpu.VMEM((2 * KV_RANK, N_HEADS * LANES), bf16),
            pltpu.VMEM((Q_BLOCK, N_IDX_HEADS * LANES), f32),
            pltpu.VMEM((Q_BLOCK, N_IDX_HEADS * LANES), f32),
            pltpu.VMEM((ATT_CHUNK, N_HEADS * LANES), f32),
            pltpu.VMEM((ATT_CHUNK, N_HEADS * LANES), f32),
            pltpu.VMEM((ATT_CHUNK, N_HEADS * LANES), bf16),
            pltpu.VMEM((ATT_CHUNK, N_HEADS * LANES), bf16),
            pltpu.VMEM((N_HEADS, LANES), f32),
            pltpu.VMEM((N_HEADS, LANES), f32),
            pltpu.VMEM((OT_ROWS, N_HEADS * LANES), f32),
            pltpu.VMEM((N_HEADS, LANES), f32),
        ],
        compiler_params=pltpu.CompilerParams(
            dimension_semantics=("arbitrary", "arbitrary"), vmem_limit_bytes=VMEM_LIMIT),
    )(qlatt, qidxt, wt, ckv, ckvt, kidx, w_uvt)

    wup = w_ffn_up[i]
    wup_c = jnp.concatenate(
        [wup[:, :D_FF].reshape(D_MODEL, N_FF_CHUNKS, FF_CHUNK),
         wup[:, D_FF:].reshape(D_MODEL, N_FF_CHUNKS, FF_CHUNK)], axis=2)
    wup_c = jnp.transpose(wup_c, (1, 0, 2)).astype(bf16)
    cwf = jnp.concatenate([ffn_conv_w[i], ffn_conv_b[i][None, :],
                           jnp.zeros((SUBLANES - 4, 2 * D_FF), f32)], axis=0)
    cwf_c = jnp.concatenate(
        [cwf[:, :D_FF].reshape(SUBLANES, N_FF_CHUNKS, FF_CHUNK),
         cwf[:, D_FF:].reshape(SUBLANES, N_FF_CHUNKS, FF_CHUNK)], axis=2)
    cwf_c = jnp.transpose(cwf_c, (1, 0, 2))
    wdown_c = w_ffn_down[i].astype(bf16)

    n_post = seq // TM_POST
    tokp = lambda w: pl.BlockSpec((1, TM_POST, w), lambda b, t: (b, t, 0))
    out = pl.pallas_call(
        _post_kernel,
        grid=(bsz, n_post),
        in_specs=[
            tokp(D_MODEL), tokp(ATT_WIDTH), tokp(D_MODEL), tokp(D_MODEL),
            pl.BlockSpec((1, 1, TM_POST, PLE_DIM), lambda b, t: (0, b, t, 0)),
            _const_spec((1, D_MODEL)), _const_spec((1, D_MODEL)),
            _const_spec((ATT_WIDTH, D_MODEL)), _const_spec((D_MODEL, D_MODEL)),
            _const_spec((1, D_MODEL)), _const_spec((1, D_MODEL)),
            _const_spec((N_FF_CHUNKS, D_MODEL, 2 * FF_CHUNK)),
            _const_spec((N_FF_CHUNKS, SUBLANES, 2 * FF_CHUNK)),
            _const_spec((D_FF, D_MODEL)),
            _const_spec((D_MODEL, D_MODEL)), _const_spec((1, D_MODEL)),
            _const_spec((PLE_DIM, D_MODEL)),
            _const_spec((1, D_MODEL)), _const_spec((1, D_MODEL)),
        ],
        out_specs=tokp(D_MODEL),
        out_shape=jax.ShapeDtypeStruct((bsz, seq, D_MODEL), x.dtype),
        scratch_shapes=[
            pltpu.VMEM((TM_POST + SUBLANES, 2 * FF_CHUNK), f32),
            pltpu.VMEM((TM_POST + SUBLANES, 2 * FF_CHUNK), f32),
            pltpu.VMEM((N_FF_CHUNKS, SUBLANES, 2 * FF_CHUNK), f32),
            pltpu.VMEM((TM_POST, D_MODEL), bf16),
            pltpu.VMEM((TM_POST, 2 * FF_CHUNK), bf16),
            pltpu.VMEM((TM_POST, D_MODEL), f32),
        ],
        compiler_params=pltpu.CompilerParams(
            dimension_semantics=("arbitrary", "arbitrary"), vmem_limit_bytes=VMEM_LIMIT),
    )(x, att, sig, mconv, p, row(ln_emb_g), row(ln_emb_b), w_br_att[i].astype(bf16),
      w_o[i].astype(bf16), row(ln1_g[i]), row(ln1_b[i]), wup_c, cwf_c, wdown_c,
      w_ple_gate[i].astype(bf16), row(b_ple_gate[i]), w_ple[i].astype(bf16),
      row(ln2_g[i]), row(ln2_b[i]))
    return out
```

```python
import functools

import jax
import jax.numpy as jnp
from jax import lax
from jax.experimental import pallas as pl
from jax.experimental.pallas import tpu as pltpu

D_MODEL = 1024
PLE_DIM = 256
N_HEADS = 8
HEAD_DIM = 64
ATT_WIDTH = N_HEADS * HEAD_DIM
KV_RANK = 128
N_IDX_HEADS = 8
IDX_DIM = 64
TOPK_MAX = 256
Q_BLOCK = 128
D_CONV = 512
D_FF = 2816
LN_EPS = 1e-5
DEPTH = 1
ALPHA = (2.0 * DEPTH) ** 0.25
ATT_SCALE = HEAD_DIM ** -0.5
IDX_SCALE = IDX_DIM ** -0.5

LANES = 128
SUBLANES = 8
FF_CHUNK = 256
N_FF_CHUNKS = D_FF // FF_CHUNK
TM_PRE = 512
TM_POST = 256
ATT_CHUNK = 256
N_CNT_ACC = 4
VMEM_LIMIT = 56 * 1024 * 1024

INT_MIN = -(2 ** 31)
INT16_MIN = -(2 ** 15)
HI_MASK = -(2 ** 16)
SIGN16_X2 = -2147450880
LOG2E = 1.4426950408889634
OT_ROWS = KV_RANK + 16
KEY_NEG_INF = -2139095041
MASK_NEG = -(2.0 ** 126)

_TOK_CKV = 0
_TOK_KIDX = 128
_TOK_CVB = 256
_TOK_CVC = _TOK_CVB + D_CONV
_TOK_CVX = _TOK_CVC + D_CONV
_TOK_GATT = _TOK_CVX + D_CONV
_TOK_GCONV = _TOK_GATT + D_MODEL
_TOK_WIDTH = _TOK_GCONV + D_MODEL


def _layer_norm(x, g, b):
    mu = jnp.mean(x, axis=-1, keepdims=True)
    d = x - mu
    var = jnp.mean(d * d, axis=-1, keepdims=True)
    return d * lax.rsqrt(var + LN_EPS) * g + b


def _dot(a, b):
    return jnp.dot(a, b, preferred_element_type=jnp.float32)


def _dot_nt(a, b):
    return lax.dot_general(a, b, (((1,), (1,)), ((), ())), preferred_element_type=jnp.float32)


def _pre_kernel(x_ref, lng_ref, lnb_ref, wtok_ref, wqt_ref, wwt_ref, kvg_ref, kig_ref, kib_ref,
                wuk_ref, cw_ref, wbrc_ref, bg_ref,
                ckv_ref, ckvt_ref, kidx_ref, qlatt_ref, qidxt_ref, wt_ref, mconv_ref, sig_ref,
                zs_ref):
    tm = x_ref.shape[1]
    nsub = tm // Q_BLOCK

    @pl.when(pl.program_id(1) == 0)
    def _():
        zs_ref[0:SUBLANES, :] = jnp.zeros((SUBLANES, D_CONV), jnp.float32)

    h = _layer_norm(x_ref[0], lng_ref[...], lnb_ref[...])
    hb = h.astype(jnp.bfloat16)
    proj = _dot(hb, wtok_ref[...])

    c = proj[:, _TOK_CKV:_TOK_CKV + KV_RANK]
    c = c * lax.rsqrt(jnp.mean(c * c, axis=-1, keepdims=True) + LN_EPS) * kvg_ref[...]
    ckv_ref[0] = c.astype(jnp.bfloat16)
    ones_row = (lax.broadcasted_iota(jnp.int32, (OT_ROWS - KV_RANK, ATT_CHUNK), 0) == 0)
    for j in range(tm // ATT_CHUNK):
        ckvt_ref[0, j, 0:KV_RANK, :] = c[j * ATT_CHUNK:(j + 1) * ATT_CHUNK, :].T.astype(jnp.bfloat16)
        ckvt_ref[0, j, KV_RANK:OT_ROWS, :] = jnp.where(ones_row, 1.0, 0.0).astype(jnp.bfloat16)

    k = proj[:, _TOK_KIDX:_TOK_KIDX + LANES]
    valid = lax.broadcasted_iota(jnp.int32, k.shape, 1) < IDX_DIM
    mu = jnp.sum(k, axis=-1, keepdims=True) * (1.0 / IDX_DIM)
    d = jnp.where(valid, k - mu, 0.0)
    var = jnp.sum(d * d, axis=-1, keepdims=True) * (1.0 / IDX_DIM)
    kn = d * lax.rsqrt(var + LN_EPS) * kig_ref[...] + kib_ref[...]
    kidx_ref[0] = kn[:, :IDX_DIM].astype(jnp.bfloat16)

    z = proj[:, _TOK_CVC:_TOK_CVC + D_CONV] * proj[:, _TOK_CVX:_TOK_CVX + D_CONV]
    zs_ref[SUBLANES:SUBLANES + tm, :] = z
    conv = (cw_ref[0:1, :] * zs_ref[SUBLANES - 2:SUBLANES - 2 + tm, :]
            + cw_ref[1:2, :] * zs_ref[SUBLANES - 1:SUBLANES - 1 + tm, :]
            + cw_ref[2:3, :] * z + cw_ref[3:4, :])
    zs_ref[0:SUBLANES, :] = z[tm - SUBLANES:tm, :]
    conv_y = proj[:, _TOK_CVB:_TOK_CVB + D_CONV] * conv
    g_conv = jax.nn.sigmoid(proj[:, _TOK_GCONV:_TOK_GCONV + D_MODEL] + bg_ref[1:2, :])
    mconv_ref[0] = (g_conv * _dot(conv_y.astype(jnp.bfloat16), wbrc_ref[...])).astype(jnp.bfloat16)
    sig_ref[0] = jax.nn.sigmoid(proj[:, _TOK_GATT:_TOK_GATT + D_MODEL] + bg_ref[0:1, :]).astype(jnp.bfloat16)

    projt = _dot_nt(wqt_ref[...], hb)
    wt = _dot_nt(wwt_ref[...], hb)[0:N_IDX_HEADS, :] * (N_IDX_HEADS ** -0.5) * IDX_SCALE
    for hd in range(N_HEADS):
        qa = projt[hd * HEAD_DIM:(hd + 1) * HEAD_DIM, :].astype(jnp.bfloat16)
        ql = (_dot(wuk_ref[hd], qa) * (ATT_SCALE * LOG2E)).astype(jnp.bfloat16)
        qi = projt[ATT_WIDTH + hd * IDX_DIM:ATT_WIDTH + (hd + 1) * IDX_DIM, :].astype(jnp.bfloat16)
        for j in range(nsub):
            qlatt_ref[0, j, :, hd * LANES:(hd + 1) * LANES] = ql[:, j * Q_BLOCK:(j + 1) * Q_BLOCK]
            qidxt_ref[0, j, :, hd * LANES:(hd + 1) * LANES] = qi[:, j * Q_BLOCK:(j + 1) * Q_BLOCK]
    for j in range(nsub):
        wt_ref[0, j] = wt[:, j * Q_BLOCK:(j + 1) * Q_BLOCK]


def _attn_kernel(qlatt_ref, qidxt_ref, wt_ref, ckv_ref, ckvt_ref, kidx_ref, wuvt_ref,
                 att_ref,
                 keys_ref, hi_ref, lo_ref, kvb_ref, qaug_ref, rawa_ref, rawb_ref, lga_ref, lgb_ref, pa_ref, pb_ref,
                 corra_ref, corrb_ref, ot_ref, m_ref, thr_ref, *, topk, idx_bits):
    blk = pl.program_id(1)
    npair = (blk + 2) // 2
    nk2 = 2 * npair
    nquad = (npair + 1) // 2
    q0 = blk * Q_BLOCK
    row_iota = lax.broadcasted_iota(jnp.int32, (Q_BLOCK, LANES), 0)
    qpos = q0 + lax.broadcasted_iota(jnp.int32, (Q_BLOCK, LANES), 1)

    def raw_dot(c):
        r0 = pl.multiple_of(c * Q_BLOCK, Q_BLOCK)
        return _dot(kidx_ref[0, pl.ds(r0, Q_BLOCK), :], qidxt_ref[0, 0])

    def score_keys(raw_ref, c):
        acc = jnp.zeros((Q_BLOCK, LANES), jnp.float32)
        for hd in range(N_IDX_HEADS):
            acc = acc + (jnp.maximum(raw_ref[:, hd * LANES:(hd + 1) * LANES], 0.0)
                         * wt_ref[0, 0, hd:hd + 1, :])
        r0 = pl.multiple_of(c * Q_BLOCK, Q_BLOCK)
        acc = jnp.where(r0 + row_iota <= qpos, acc, -jnp.inf)
        bits = lax.bitcast_convert_type(acc, jnp.int32)
        key = bits ^ ((bits >> 31) & 0x7FFFFFFF)
        keys_ref[pl.ds(r0, Q_BLOCK), :] = key
        return key

    @pl.when(blk == 0)
    def _():
        kvb_ref[:, 0:KV_RANK] = ckv_ref[0]
        ident = jnp.where(row_iota == lax.broadcasted_iota(jnp.int32, (Q_BLOCK, LANES), 1), 1.0, 0.0)
        for hd in range(N_HEADS):
            qaug_ref[KV_RANK:2 * KV_RANK, hd * LANES:(hd + 1) * LANES] = ident.astype(jnp.bfloat16)

    qaug_ref[0:KV_RANK, :] = qlatt_ref[0, 0]

    @pl.when(npair % 2 == 1)
    def _():
        r0 = pl.multiple_of(npair * Q_BLOCK, Q_BLOCK)
        hi_ref[pl.ds(r0, Q_BLOCK), :] = jnp.full((Q_BLOCK, LANES), SIGN16_X2, jnp.int32)
        lo_ref[pl.ds(r0, Q_BLOCK), :] = jnp.full((Q_BLOCK, LANES), SIGN16_X2, jnp.int32)
        r1 = pl.multiple_of(npair * ATT_CHUNK, ATT_CHUNK)
        kvb_ref[pl.ds(r1, ATT_CHUNK), KV_RANK:2 * KV_RANK] = jnp.full(
            (ATT_CHUNK, LANES), MASK_NEG, jnp.bfloat16)

    rawa_ref[...] = raw_dot(0)

    def score_pair(i, carry):
        rawb_ref[...] = raw_dot(2 * i + 1)
        ka = score_keys(rawa_ref, 2 * i)
        rawa_ref[...] = raw_dot(jnp.minimum(2 * i + 2, nk2 - 2))
        kb = score_keys(rawb_ref, 2 * i + 1)
        r0 = pl.multiple_of(i * Q_BLOCK, Q_BLOCK)
        hi_ref[pl.ds(r0, Q_BLOCK), :] = lax.shift_right_logical(ka, 16) | (kb & HI_MASK)
        lo_ref[pl.ds(r0, Q_BLOCK), :] = ((ka & 0xFFFF) | lax.shift_left(kb, 16)) ^ SIGN16_X2
        return carry

    lax.fori_loop(0, npair, score_pair, 0)

    def threshold(n_steps):
        def count16(ref, cand):
            c32 = (cand & 0xFFFF) | lax.shift_left(cand, 16)
            c16 = pltpu.bitcast(jnp.broadcast_to(c32, (SUBLANES, LANES)), jnp.int16)
            accs = [jnp.zeros((16, LANES), jnp.int16)] * N_CNT_ACC
            for j in range(n_steps):
                k = pltpu.bitcast(ref[j * ATT_CHUNK:(j + 1) * ATT_CHUNK, :], jnp.int16)
                ones = jnp.where(k.reshape(32, 16, LANES) >= c16[None], jnp.int16(1), jnp.int16(0))
                for g in range(32):
                    accs[g % N_CNT_ACC] = accs[g % N_CNT_ACC] + ones[g]
            acc = (accs[0] + accs[1]) + (accs[2] + accs[3])
            a32 = pltpu.bitcast(acc, jnp.int32)
            return jnp.sum((a32 & 0xFFFF) + lax.shift_right_logical(a32, 16), axis=0, keepdims=True)

        def search16(ref, need):
            def bit_body(i, carry):
                t, above = carry
                cand = t + lax.shift_left(jnp.int32(1), 15 - i)
                cnt = count16(ref, cand)
                take = cnt >= need
                return jnp.where(take, cand, t), jnp.where(take, above, cnt)
            return lax.fori_loop(0, 16, bit_body, (jnp.full((1, LANES), INT16_MIN, jnp.int32),
                                                   jnp.zeros((1, LANES), jnp.int32)))

        t_hi, above_hi = search16(hi_ref, topk)

        t_hi32 = (t_hi & 0xFFFF) | lax.shift_left(t_hi, 16)
        t_hi16 = pltpu.bitcast(jnp.broadcast_to(t_hi32, (SUBLANES, LANES)), jnp.int16)

        def lo_prep(j, carry):
            r0 = pl.multiple_of(j * Q_BLOCK, Q_BLOCK)
            h16 = pltpu.bitcast(hi_ref[pl.ds(r0, Q_BLOCK), :], jnp.int16).reshape(16, 16, LANES)
            l16 = pltpu.bitcast(lo_ref[pl.ds(r0, Q_BLOCK), :], jnp.int16).reshape(16, 16, LANES)
            kept = jnp.where(h16 == t_hi16[None], l16, jnp.int16(INT16_MIN)).reshape(2 * Q_BLOCK, LANES)
            lo_ref[pl.ds(r0, Q_BLOCK), :] = pltpu.bitcast(kept, jnp.int32)
            return carry

        lax.fori_loop(0, 2 * n_steps, lo_prep, 0)
        t_lo, _ = search16(lo_ref, topk - above_hi)
        thr_ref[...] = lax.shift_left(t_hi, 16) | (t_lo - INT16_MIN)

    for n_steps in range(1, hi_ref.shape[0] // ATT_CHUNK + 1):
        pl.when(nquad == n_steps)(functools.partial(threshold, n_steps))

    thr = jnp.maximum(thr_ref[...], KEY_NEG_INF + 1)

    def count32(pred):
        def body(j, acc):
            r0 = pl.multiple_of(j * Q_BLOCK, Q_BLOCK)
            idx = r0 + row_iota
            return acc + jnp.where(pred(keys_ref[pl.ds(r0, Q_BLOCK), :], idx), 1, 0)
        acc = lax.fori_loop(0, nk2, body, jnp.zeros((Q_BLOCK, LANES), jnp.int32))
        return jnp.sum(acc, axis=0, keepdims=True)

    def build_bias():
        def body(j, acc):
            r0 = pl.multiple_of(j * Q_BLOCK, Q_BLOCK)
            sel = keys_ref[pl.ds(r0, Q_BLOCK), :] >= thr
            kvb_ref[pl.ds(r0, Q_BLOCK), KV_RANK:2 * KV_RANK] = jnp.where(sel, 0.0, MASK_NEG).astype(jnp.bfloat16)
            return acc + jnp.where(sel, 1, 0)
        acc = lax.fori_loop(0, nk2, body, jnp.zeros((Q_BLOCK, LANES), jnp.int32))
        return jnp.sum(acc, axis=0, keepdims=True)

    n_sel = build_bias()

    @pl.when(jnp.max(n_sel) > topk)
    def _():
        need = topk - count32(lambda k, idx: k > thr)

        def jbit(i, jmax):
            cand = jmax + lax.shift_left(jnp.int32(1), idx_bits - 1 - i)
            f = count32(lambda k, idx: jnp.where(k == thr, idx, cand) < cand)
            return jnp.where(f < need, cand, jmax)

        jmax = lax.fori_loop(0, idx_bits, jbit, jnp.zeros((1, LANES), jnp.int32))

        def drop(j, carry):
            r0 = pl.multiple_of(j * Q_BLOCK, Q_BLOCK)
            k = keys_ref[pl.ds(r0, Q_BLOCK), :]
            tie_idx = jnp.where(k == thr, r0 + row_iota, 0)
            old = kvb_ref[pl.ds(r0, Q_BLOCK), KV_RANK:2 * KV_RANK].astype(jnp.float32)
            kvb_ref[pl.ds(r0, Q_BLOCK), KV_RANK:2 * KV_RANK] = jnp.where(
                tie_idx > jmax, MASK_NEG, old).astype(jnp.bfloat16)
            return carry

        lax.fori_loop(0, nk2, drop, 0)

    def qk_dot(c):
        r0 = pl.multiple_of(c * ATT_CHUNK, ATT_CHUNK)
        return _dot(kvb_ref[pl.ds(r0, ATT_CHUNK), :], qaug_ref[...])

    def pv_update(c, p_ref, corr_ref):
        pv = _dot(ckvt_ref[0, c], p_ref[...])
        for hd in range(N_HEADS):
            lanes = slice(hd * LANES, (hd + 1) * LANES)
            ot_ref[:, lanes] = ot_ref[:, lanes] * corr_ref[hd:hd + 1, :] + pv[:, lanes]

    def softmax_chunk(lg_ref, p_ref, corr_ref):
        for hd in range(N_HEADS):
            lanes = slice(hd * LANES, (hd + 1) * LANES)
            logit = lg_ref[:, lanes]
            m_old = m_ref[hd:hd + 1, :]
            m_new = jnp.maximum(m_old, jnp.max(logit, axis=0, keepdims=True))
            p_ref[:, lanes] = jnp.exp2(logit - m_new).astype(jnp.bfloat16)
            corr_ref[hd:hd + 1, :] = jnp.exp2(m_old - m_new)
            m_ref[hd:hd + 1, :] = m_new

    m_ref[...] = jnp.full(m_ref.shape, -jnp.inf, jnp.float32)
    ot_ref[...] = jnp.zeros(ot_ref.shape, jnp.float32)
    pb_ref[...] = jnp.zeros(pb_ref.shape, jnp.bfloat16)
    corrb_ref[...] = jnp.ones(corrb_ref.shape, jnp.float32)
    lga_ref[...] = qk_dot(0)
    last = 2 * nquad - 1

    def att_pair(i, carry):
        c0 = 2 * i
        lgb_ref[...] = qk_dot(c0 + 1)
        pv_update(jnp.maximum(c0 - 1, 0), pb_ref, corrb_ref)
        softmax_chunk(lga_ref, pa_ref, corra_ref)
        lga_ref[...] = qk_dot(jnp.minimum(c0 + 2, last))
        pv_update(c0, pa_ref, corra_ref)
        softmax_chunk(lgb_ref, pb_ref, corrb_ref)
        return carry

    lax.fori_loop(0, nquad, att_pair, 0)
    pv_update(last, pb_ref, corrb_ref)

    outs = []
    for hd in range(N_HEADS):
        lanes = slice(hd * LANES, (hd + 1) * LANES)
        o = (ot_ref[0:KV_RANK, lanes] / ot_ref[KV_RANK:KV_RANK + 1, lanes]).astype(jnp.bfloat16)
        outs.append(_dot(wuvt_ref[hd], o))
    att_t = jnp.concatenate(outs, axis=0)
    att_ref[0] = att_t.T.astype(jnp.bfloat16)


def _post_kernel(x_ref, att_ref, sig_ref, mconv_ref, p_ref, lng_ref, lnb_ref, wbra_ref, wo_ref,
                 ln1g_ref, ln1b_ref, wup_ref, cw_ref, wdown_ref, wpg_ref, bpg_ref, wple_ref,
                 ln2g_ref, ln2b_ref,
                 out_ref,
                 prea_ref, preb_ref, carry_ref, h1b_ref, act_ref, acc_ref):
    tm = x_ref.shape[1]
    n_pairs = N_FF_CHUNKS // 2

    @pl.when(pl.program_id(1) == 0)
    def _():
        carry_ref[...] = jnp.zeros(carry_ref.shape, jnp.float32)

    h = _layer_norm(x_ref[0], lng_ref[...], lnb_ref[...])
    merged = (sig_ref[0].astype(jnp.float32) * _dot(att_ref[0], wbra_ref[...])
              + mconv_ref[0].astype(jnp.float32))
    y = _dot(merged.astype(jnp.bfloat16), wo_ref[...])
    h1 = _layer_norm(ALPHA * h + y, ln1g_ref[...], ln1b_ref[...])
    h1b_ref[...] = h1.astype(jnp.bfloat16)
    ple = (jax.nn.sigmoid(_dot(h1b_ref[...], wpg_ref[...]) + bpg_ref[...])
           * _dot(p_ref[0, 0].astype(jnp.bfloat16), wple_ref[...]))

    def up_dot(j):
        return _dot(h1b_ref[...], wup_ref[j])

    def conv_act(pre_ref, j):
        pre_ref[0:SUBLANES, :] = carry_ref[j]
        cw = cw_ref[j]
        pre = pre_ref[SUBLANES:SUBLANES + tm, :]
        conv = (cw[0:1, :] * pre_ref[SUBLANES - 2:SUBLANES - 2 + tm, :]
                + cw[1:2, :] * pre_ref[SUBLANES - 1:SUBLANES - 1 + tm, :]
                + cw[2:3, :] * pre + cw[3:4, :])
        carry_ref[j] = pre[tm - SUBLANES:tm, :]
        g = conv[:, :FF_CHUNK]
        u = conv[:, FF_CHUNK:]
        return (g * jax.nn.sigmoid(g) * u).astype(jnp.bfloat16)

    prea_ref[SUBLANES:SUBLANES + tm, :] = up_dot(0)
    act_ref[...] = jnp.zeros(act_ref.shape, jnp.bfloat16)
    acc_ref[...] = jnp.zeros(acc_ref.shape, jnp.float32)

    def ff_pair(i, carry):
        j0 = 2 * i
        r0 = pl.multiple_of(jnp.maximum(i - 1, 0) * (2 * FF_CHUNK), 2 * FF_CHUNK)
        acc_ref[...] += _dot(act_ref[...], wdown_ref[pl.ds(r0, 2 * FF_CHUNK), :])
        preb_ref[SUBLANES:SUBLANES + tm, :] = up_dot(j0 + 1)
        act_ref[:, 0:FF_CHUNK] = conv_act(prea_ref, j0)
        prea_ref[SUBLANES:SUBLANES + tm, :] = up_dot(jnp.minimum(j0 + 2, N_FF_CHUNKS - 1))
        act_ref[:, FF_CHUNK:2 * FF_CHUNK] = conv_act(preb_ref, j0 + 1)
        return carry

    lax.fori_loop(0, n_pairs, ff_pair, 0)
    ffn = acc_ref[...] + _dot(act_ref[...], wdown_ref[(n_pairs - 1) * 2 * FF_CHUNK:n_pairs * 2 * FF_CHUNK, :])
    if N_FF_CHUNKS % 2:
        ffn = ffn + _dot(conv_act(prea_ref, N_FF_CHUNKS - 1), wdown_ref[n_pairs * 2 * FF_CHUNK:D_FF, :])

    out_ref[0] = _layer_norm(ALPHA * h1 + ffn + ple, ln2g_ref[...], ln2b_ref[...])


def _const_spec(shape):
    nd = len(shape)
    return pl.BlockSpec(shape, lambda *_: (0,) * nd, pipeline_mode=pl.Buffered(1))


def kernel(x, p, ln_emb_g, ln_emb_b, w_in, b_gate, kv_norm_g, w_uk, w_uv, k_idx_ln_g, k_idx_ln_b,
           mix_conv_w, mix_conv_b, w_br_att, w_br_conv, w_o, ln1_g, ln1_b, w_ffn_up, ffn_conv_w,
           ffn_conv_b, w_ffn_down, w_ple_gate, b_ple_gate, w_ple, ln2_g, ln2_b):
    bsz, seq, _ = x.shape
    assert seq % TM_PRE == 0 and seq % TM_POST == 0 and TM_PRE % (2 * ATT_CHUNK) == 0
    nb = seq // Q_BLOCK
    topk = min(TOPK_MAX, seq // 4)
    idx_bits = max(1, (seq - 1).bit_length())
    bf16, f32 = jnp.bfloat16, jnp.float32
    i = 0

    o_qatt = 0
    o_ckv = o_qatt + ATT_WIDTH
    o_qidx = o_ckv + KV_RANK
    o_kidx = o_qidx + N_IDX_HEADS * IDX_DIM
    o_widx = o_kidx + IDX_DIM
    o_cvb = o_widx + N_IDX_HEADS
    win = w_in[i]
    pad64 = jnp.zeros((D_MODEL, LANES - IDX_DIM), f32)
    w_tok = jnp.concatenate(
        [win[:, o_ckv:o_ckv + KV_RANK], win[:, o_kidx:o_kidx + IDX_DIM], pad64, win[:, o_cvb:]],
        axis=1).astype(bf16)
    w_qt = jnp.concatenate([win[:, o_qatt:o_qatt + ATT_WIDTH],
                            win[:, o_qidx:o_qidx + N_IDX_HEADS * IDX_DIM]], axis=1).T.astype(bf16)
    w_wt = jnp.concatenate([win[:, o_widx:o_widx + N_IDX_HEADS].T,
                            jnp.zeros((16 - N_IDX_HEADS, D_MODEL), f32)], axis=0).astype(bf16)
    row = lambda v: v.reshape(1, -1).astype(f32)
    kig = jnp.concatenate([k_idx_ln_g[i], jnp.zeros((LANES - IDX_DIM,), f32)]).reshape(1, LANES)
    kib = jnp.concatenate([k_idx_ln_b[i], jnp.zeros((LANES - IDX_DIM,), f32)]).reshape(1, LANES)
    cw_mix = jnp.concatenate([mix_conv_w[i], mix_conv_b[i][None, :],
                              jnp.zeros((SUBLANES - 4, D_CONV), f32)], axis=0)

    n_pre = seq // TM_PRE
    sub_pre = TM_PRE // Q_BLOCK
    tok3 = lambda w: pl.BlockSpec((1, TM_PRE, w), lambda b, t: (b, t, 0))
    blk4 = lambda r, w: pl.BlockSpec((1, sub_pre, r, w), lambda b, t: (b, t, 0, 0))
    pre_out_shapes = (
        jax.ShapeDtypeStruct((bsz, seq, KV_RANK), bf16),
        jax.ShapeDtypeStruct((bsz, seq // ATT_CHUNK, OT_ROWS, ATT_CHUNK), bf16),
        jax.ShapeDtypeStruct((bsz, seq, IDX_DIM), bf16),
        jax.ShapeDtypeStruct((bsz, nb, KV_RANK, N_HEADS * LANES), bf16),
        jax.ShapeDtypeStruct((bsz, nb, IDX_DIM, N_IDX_HEADS * LANES), bf16),
        jax.ShapeDtypeStruct((bsz, nb, N_IDX_HEADS, Q_BLOCK), f32),
        jax.ShapeDtypeStruct((bsz, seq, D_MODEL), bf16),
        jax.ShapeDtypeStruct((bsz, seq, D_MODEL), bf16),
    )
    ckv, ckvt, kidx, qlatt, qidxt, wt, mconv, sig = pl.pallas_call(
        _pre_kernel,
        grid=(bsz, n_pre),
        in_specs=[
            tok3(D_MODEL),
            _const_spec((1, D_MODEL)), _const_spec((1, D_MODEL)),
            _const_spec((D_MODEL, _TOK_WIDTH)), _const_spec((2 * ATT_WIDTH, D_MODEL)),
            _const_spec((16, D_MODEL)),
            _const_spec((1, KV_RANK)), _const_spec((1, LANES)), _const_spec((1, LANES)),
            _const_spec((N_HEADS, KV_RANK, HEAD_DIM)),
            _const_spec((SUBLANES, D_CONV)), _const_spec((D_CONV, D_MODEL)), _const_spec((2, D_MODEL)),
        ],
        out_specs=(
            tok3(KV_RANK),
            pl.BlockSpec((1, TM_PRE // ATT_CHUNK, OT_ROWS, ATT_CHUNK), lambda b, t: (b, t, 0, 0)),
            tok3(IDX_DIM),
            blk4(KV_RANK, N_HEADS * LANES), blk4(IDX_DIM, N_IDX_HEADS * LANES),
            blk4(N_IDX_HEADS, Q_BLOCK), tok3(D_MODEL), tok3(D_MODEL),
        ),
        out_shape=pre_out_shapes,
        scratch_shapes=[pltpu.VMEM((TM_PRE + SUBLANES, D_CONV), f32)],
        compiler_params=pltpu.CompilerParams(
            dimension_semantics=("arbitrary", "arbitrary"), vmem_limit_bytes=VMEM_LIMIT),
    )(x, row(ln_emb_g), row(ln_emb_b), w_tok, w_qt, w_wt, row(kv_norm_g[i]), kig, kib,
      w_uk[i].astype(bf16), cw_mix, w_br_conv[i].astype(bf16), b_gate[i].astype(f32))

    w_uvt = jnp.swapaxes(w_uv[i], 1, 2).astype(bf16)
    att = pl.pallas_call(
        functools.partial(_attn_kernel, topk=topk, idx_bits=idx_bits),
        grid=(bsz, nb),
        in_specs=[
            pl.BlockSpec((1, 1, KV_RANK, N_HEADS * LANES), lambda b, q: (b, q, 0, 0)),
            pl.BlockSpec((1, 1, IDX_DIM, N_IDX_HEADS * LANES), lambda b, q: (b, q, 0, 0)),
            pl.BlockSpec((1, 1, N_IDX_HEADS, Q_BLOCK), lambda b, q: (b, q, 0, 0)),
            pl.BlockSpec((1, seq, KV_RANK), lambda b, q: (b, 0, 0)),
            pl.BlockSpec((1, seq // ATT_CHUNK, OT_ROWS, ATT_CHUNK), lambda b, q: (b, 0, 0, 0)),
            pl.BlockSpec((1, seq, IDX_DIM), lambda b, q: (b, 0, 0)),
            _const_spec((N_HEADS, HEAD_DIM, KV_RANK)),
        ],
        out_specs=pl.BlockSpec((1, Q_BLOCK, ATT_WIDTH), lambda b, q: (b, q, 0)),
        out_shape=jax.ShapeDtypeStruct((bsz, seq, ATT_WIDTH), bf16),
        scratch_shapes=[
            pltpu.VMEM((seq, LANES), jnp.int32),
            pltpu.VMEM((seq // 2, LANES), jnp.int32),
            pltpu.VMEM((seq // 2, LANES), jnp.int32),
            pltpu.VMEM((seq, 2 * KV_RANK), bf16),
            pltpu.VMEM((2 * KV_RANK, N_HEADS * LANES), bf16),
            pltpu.VMEM((Q_BLOCK, N_IDX_HEADS * LANES), f32),
            pltpu.VMEM((Q_BLOCK, N_IDX_HEADS * LANES), f32),
            pltpu.VMEM((ATT_CHUNK, N_HEADS * LANES), f32),
            pltpu.VMEM((ATT_CHUNK, N_HEADS * LANES), f32),
            pltpu.VMEM((ATT_CHUNK, N_HEADS * LANES), bf16),
            pltpu.VMEM((ATT_CHUNK, N_HEADS * LANES), bf16),
            pltpu.VMEM((N_HEADS, LANES), f32),
            pltpu.VMEM((N_HEADS, LANES), f32),
            pltpu.VMEM((OT_ROWS, N_HEADS * LANES), f32),
            pltpu.VMEM((N_HEADS, LANES), f32),
            pltpu.VMEM((1, LANES), jnp.int32),
        ],
        compiler_params=pltpu.CompilerParams(
            dimension_semantics=("arbitrary", "arbitrary"), vmem_limit_bytes=VMEM_LIMIT),
    )(qlatt, qidxt, wt, ckv, ckvt, kidx, w_uvt)

    wup = w_ffn_up[i]
    wup_c = jnp.concatenate(
        [wup[:, :D_FF].reshape(D_MODEL, N_FF_CHUNKS, FF_CHUNK),
         wup[:, D_FF:].reshape(D_MODEL, N_FF_CHUNKS, FF_CHUNK)], axis=2)
    wup_c = jnp.transpose(wup_c, (1, 0, 2)).astype(bf16)
    cwf = jnp.concatenate([ffn_conv_w[i], ffn_conv_b[i][None, :],
                           jnp.zeros((SUBLANES - 4, 2 * D_FF), f32)], axis=0)
    cwf_c = jnp.concatenate(
        [cwf[:, :D_FF].reshape(SUBLANES, N_FF_CHUNKS, FF_CHUNK),
         cwf[:, D_FF:].reshape(SUBLANES, N_FF_CHUNKS, FF_CHUNK)], axis=2)
    cwf_c = jnp.transpose(cwf_c, (1, 0, 2))
    wdown_c = w_ffn_down[i].astype(bf16)

    n_post = seq // TM_POST
    tokp = lambda w: pl.BlockSpec((1, TM_POST, w), lambda b, t: (b, t, 0))
    out = pl.pallas_call(
        _post_kernel,
        grid=(bsz, n_post),
        in_specs=[
            tokp(D_MODEL), tokp(ATT_WIDTH), tokp(D_MODEL), tokp(D_MODEL),
            pl.BlockSpec((1, 1, TM_POST, PLE_DIM), lambda b, t: (0, b, t, 0)),
            _const_spec((1, D_MODEL)), _const_spec((1, D_MODEL)),
            _const_spec((ATT_WIDTH, D_MODEL)), _const_spec((D_MODEL, D_MODEL)),
            _const_spec((1, D_MODEL)), _const_spec((1, D_MODEL)),
            _const_spec((N_FF_CHUNKS, D_MODEL, 2 * FF_CHUNK)),
            _const_spec((N_FF_CHUNKS, SUBLANES, 2 * FF_CHUNK)),
            _const_spec((D_FF, D_MODEL)),
            _const_spec((D_MODEL, D_MODEL)), _const_spec((1, D_MODEL)),
            _const_spec((PLE_DIM, D_MODEL)),
            _const_spec((1, D_MODEL)), _const_spec((1, D_MODEL)),
        ],
        out_specs=tokp(D_MODEL),
        out_shape=jax.ShapeDtypeStruct((bsz, seq, D_MODEL), x.dtype),
        scratch_shapes=[
            pltpu.VMEM((TM_POST + SUBLANES, 2 * FF_CHUNK), f32),
            pltpu.VMEM((TM_POST + SUBLANES, 2 * FF_CHUNK), f32),
            pltpu.VMEM((N_FF_CHUNKS, SUBLANES, 2 * FF_CHUNK), f32),
            pltpu.VMEM((TM_POST, D_MODEL), bf16),
            pltpu.VMEM((TM_POST, 2 * FF_CHUNK), bf16),
            pltpu.VMEM((TM_POST, D_MODEL), f32),
        ],
        compiler_params=pltpu.CompilerParams(
            dimension_semantics=("arbitrary", "arbitrary"), vmem_limit_bytes=VMEM_LIMIT),
    )(x, att, sig, mconv, p, row(ln_emb_g), row(ln_emb_b), w_br_att[i].astype(bf16),
      w_o[i].astype(bf16), row(ln1_g[i]), row(ln1_b[i]), wup_c, cwf_c, wdown_c,
      w_ple_gate[i].astype(bf16), row(b_ple_gate[i]), w_ple[i].astype(bf16),
      row(ln2_g[i]), row(ln2_b[i]))
    return out
```

```python
import functools

import jax
import jax.numpy as jnp
from jax import lax
from jax.experimental import pallas as pl
from jax.experimental.pallas import tpu as pltpu

D_MODEL = 1024
PLE_DIM = 256
N_HEADS = 8
HEAD_DIM = 64
ATT_WIDTH = N_HEADS * HEAD_DIM
KV_RANK = 128
N_IDX_HEADS = 8
IDX_DIM = 64
TOPK_MAX = 256
Q_BLOCK = 128
D_CONV = 512
D_FF = 2816
LN_EPS = 1e-5
DEPTH = 1
ALPHA = (2.0 * DEPTH) ** 0.25
ATT_SCALE = HEAD_DIM ** -0.5
IDX_SCALE = IDX_DIM ** -0.5

LANES = 128
SUBLANES = 8
FF_CHUNK = 256
N_FF_CHUNKS = D_FF // FF_CHUNK
TM_PRE = 512
TM_POST = 256
ATT_CHUNK = 256
N_CNT_ACC = 4
VMEM_LIMIT = 56 * 1024 * 1024

INT_MIN = -(2 ** 31)
INT16_MIN = -(2 ** 15)
HI_MASK = -(2 ** 16)
SIGN16_X2 = -2147450880
LOG2E = 1.4426950408889634
OT_ROWS = KV_RANK + 16
KEY_NEG_INF = -2139095041
MASK_NEG = -(2.0 ** 126)

_TOK_CKV = 0
_TOK_KIDX = 128
_TOK_CVB = 256
_TOK_CVC = _TOK_CVB + D_CONV
_TOK_CVX = _TOK_CVC + D_CONV
_TOK_GATT = _TOK_CVX + D_CONV
_TOK_GCONV = _TOK_GATT + D_MODEL
_TOK_WIDTH = _TOK_GCONV + D_MODEL


def _layer_norm(x, g, b):
    mu = jnp.mean(x, axis=-1, keepdims=True)
    d = x - mu
    var = jnp.mean(d * d, axis=-1, keepdims=True)
    return d * lax.rsqrt(var + LN_EPS) * g + b


def _dot(a, b):
    return jnp.dot(a, b, preferred_element_type=jnp.float32)


def _dot_nt(a, b):
    return lax.dot_general(a, b, (((1,), (1,)), ((), ())), preferred_element_type=jnp.float32)


def _pre_kernel(x_ref, lng_ref, lnb_ref, wtok_ref, wqt_ref, wwt_ref, kvg_ref, kig_ref, kib_ref,
                wuk_ref, cw_ref, wbrc_ref, bg_ref,
                ckv_ref, ckvt_ref, kidx_ref, qlatt_ref, qidxt_ref, wt_ref, mconv_ref, sig_ref,
                zs_ref):
    tm = x_ref.shape[1]
    nsub = tm // Q_BLOCK

    @pl.when(pl.program_id(1) == 0)
    def _():
        zs_ref[0:SUBLANES, :] = jnp.zeros((SUBLANES, D_CONV), jnp.float32)

    h = _layer_norm(x_ref[0], lng_ref[...], lnb_ref[...])
    hb = h.astype(jnp.bfloat16)
    proj = _dot(hb, wtok_ref[...])

    c = proj[:, _TOK_CKV:_TOK_CKV + KV_RANK]
    c = c * lax.rsqrt(jnp.mean(c * c, axis=-1, keepdims=True) + LN_EPS) * kvg_ref[...]
    ckv_ref[0] = c.astype(jnp.bfloat16)
    ones_row = (lax.broadcasted_iota(jnp.int32, (OT_ROWS - KV_RANK, ATT_CHUNK), 0) == 0)
    for j in range(tm // ATT_CHUNK):
        ckvt_ref[0, j, 0:KV_RANK, :] = c[j * ATT_CHUNK:(j + 1) * ATT_CHUNK, :].T.astype(jnp.bfloat16)
        ckvt_ref[0, j, KV_RANK:OT_ROWS, :] = jnp.where(ones_row, 1.0, 0.0).astype(jnp.bfloat16)

    k = proj[:, _TOK_KIDX:_TOK_KIDX + LANES]
    valid = lax.broadcasted_iota(jnp.int32, k.shape, 1) < IDX_DIM
    mu = jnp.sum(k, axis=-1, keepdims=True) * (1.0 / IDX_DIM)
    d = jnp.where(valid, k - mu, 0.0)
    var = jnp.sum(d * d, axis=-1, keepdims=True) * (1.0 / IDX_DIM)
    kn = d * lax.rsqrt(var + LN_EPS) * kig_ref[...] + kib_ref[...]
    kidx_ref[0] = kn[:, :IDX_DIM].astype(jnp.bfloat16)

    z = proj[:, _TOK_CVC:_TOK_CVC + D_CONV] * proj[:, _TOK_CVX:_TOK_CVX + D_CONV]
    zs_ref[SUBLANES:SUBLANES + tm, :] = z
    conv = (cw_ref[0:1, :] * zs_ref[SUBLANES - 2:SUBLANES - 2 + tm, :]
            + cw_ref[1:2, :] * zs_ref[SUBLANES - 1:SUBLANES - 1 + tm, :]
            + cw_ref[2:3, :] * z + cw_ref[3:4, :])
    zs_ref[0:SUBLANES, :] = z[tm - SUBLANES:tm, :]
    conv_y = proj[:, _TOK_CVB:_TOK_CVB + D_CONV] * conv
    g_conv = jax.nn.sigmoid(proj[:, _TOK_GCONV:_TOK_GCONV + D_MODEL] + bg_ref[1:2, :])
    mconv_ref[0] = (g_conv * _dot(conv_y.astype(jnp.bfloat16), wbrc_ref[...])).astype(jnp.bfloat16)
    sig_ref[0] = jax.nn.sigmoid(proj[:, _TOK_GATT:_TOK_GATT + D_MODEL] + bg_ref[0:1, :]).astype(jnp.bfloat16)

    projt = _dot_nt(wqt_ref[...], hb)
    wt = _dot_nt(wwt_ref[...], hb)[0:N_IDX_HEADS, :] * (N_IDX_HEADS ** -0.5) * IDX_SCALE
    for hd in range(N_HEADS):
        qa = projt[hd * HEAD_DIM:(hd + 1) * HEAD_DIM, :].astype(jnp.bfloat16)
        ql = (_dot(wuk_ref[hd], qa) * (ATT_SCALE * LOG2E)).astype(jnp.bfloat16)
        qi = projt[ATT_WIDTH + hd * IDX_DIM:ATT_WIDTH + (hd + 1) * IDX_DIM, :].astype(jnp.bfloat16)
        for j in range(nsub):
            qlatt_ref[0, j, :, hd * LANES:(hd + 1) * LANES] = ql[:, j * Q_BLOCK:(j + 1) * Q_BLOCK]
            qidxt_ref[0, j, :, hd * LANES:(hd + 1) * LANES] = qi[:, j * Q_BLOCK:(j + 1) * Q_BLOCK]
    for j in range(nsub):
        wt_ref[0, j] = wt[:, j * Q_BLOCK:(j + 1) * Q_BLOCK]


def _attn_kernel(qlatt_ref, qidxt_ref, wt_ref, ckv_ref, ckvt_ref, kidx_ref, wuvt_ref,
                 att_ref,
                 keys_ref, hi_ref, lo_ref, kvb_ref, qaug_ref, rawa_ref, rawb_ref, lga_ref, lgb_ref, pa_ref, pb_ref,
                 corra_ref, corrb_ref, mxa_ref, mxb_ref, ot_ref, m_ref, thr_ref, *, topk, idx_bits):
    blk = pl.program_id(1)
    npair = (blk + 2) // 2
    nk2 = 2 * npair
    nquad = (npair + 1) // 2
    q0 = blk * Q_BLOCK
    row_iota = lax.broadcasted_iota(jnp.int32, (Q_BLOCK, LANES), 0)
    qpos = q0 + lax.broadcasted_iota(jnp.int32, (Q_BLOCK, LANES), 1)

    def raw_dot(c):
        r0 = pl.multiple_of(c * Q_BLOCK, Q_BLOCK)
        return _dot(kidx_ref[0, pl.ds(r0, Q_BLOCK), :], qidxt_ref[0, 0])

    def score_keys(raw_ref, c):
        acc = jnp.zeros((Q_BLOCK, LANES), jnp.float32)
        for hd in range(N_IDX_HEADS):
            acc = acc + (jnp.maximum(raw_ref[:, hd * LANES:(hd + 1) * LANES], 0.0)
                         * wt_ref[0, 0, hd:hd + 1, :])
        r0 = pl.multiple_of(c * Q_BLOCK, Q_BLOCK)
        acc = jnp.where(r0 + row_iota <= qpos, acc, -jnp.inf)
        bits = lax.bitcast_convert_type(acc, jnp.int32)
        key = bits ^ ((bits >> 31) & 0x7FFFFFFF)
        keys_ref[pl.ds(r0, Q_BLOCK), :] = key
        return key

    @pl.when(blk == 0)
    def _():
        kvb_ref[:, 0:KV_RANK] = ckv_ref[0]
        ident = jnp.where(row_iota == lax.broadcasted_iota(jnp.int32, (Q_BLOCK, LANES), 1), 1.0, 0.0)
        for hd in range(N_HEADS):
            qaug_ref[KV_RANK:2 * KV_RANK, hd * LANES:(hd + 1) * LANES] = ident.astype(jnp.bfloat16)

    qaug_ref[0:KV_RANK, :] = qlatt_ref[0, 0]

    @pl.when(npair % 2 == 1)
    def _():
        r0 = pl.multiple_of(npair * Q_BLOCK, Q_BLOCK)
        hi_ref[pl.ds(r0, Q_BLOCK), :] = jnp.full((Q_BLOCK, LANES), SIGN16_X2, jnp.int32)
        lo_ref[pl.ds(r0, Q_BLOCK), :] = jnp.full((Q_BLOCK, LANES), SIGN16_X2, jnp.int32)
        r1 = pl.multiple_of(npair * ATT_CHUNK, ATT_CHUNK)
        kvb_ref[pl.ds(r1, ATT_CHUNK), KV_RANK:2 * KV_RANK] = jnp.full(
            (ATT_CHUNK, LANES), MASK_NEG, jnp.bfloat16)

    rawa_ref[...] = raw_dot(0)

    def score_pair(i, carry):
        rawb_ref[...] = raw_dot(2 * i + 1)
        ka = score_keys(rawa_ref, 2 * i)
        rawa_ref[...] = raw_dot(jnp.minimum(2 * i + 2, nk2 - 2))
        kb = score_keys(rawb_ref, 2 * i + 1)
        r0 = pl.multiple_of(i * Q_BLOCK, Q_BLOCK)
        hi_ref[pl.ds(r0, Q_BLOCK), :] = lax.shift_right_logical(ka, 16) | (kb & HI_MASK)
        lo_ref[pl.ds(r0, Q_BLOCK), :] = ((ka & 0xFFFF) | lax.shift_left(kb, 16)) ^ SIGN16_X2
        return carry

    lax.fori_loop(0, npair, score_pair, 0)

    def threshold(n_steps):
        def count16(ref, cand):
            c32 = (cand & 0xFFFF) | lax.shift_left(cand, 16)
            c16 = pltpu.bitcast(jnp.broadcast_to(c32, (SUBLANES, LANES)), jnp.int16)
            accs = [jnp.zeros((16, LANES), jnp.int16)] * N_CNT_ACC
            for j in range(n_steps):
                k = pltpu.bitcast(ref[j * ATT_CHUNK:(j + 1) * ATT_CHUNK, :], jnp.int16)
                ones = jnp.where(k.reshape(32, 16, LANES) >= c16[None], jnp.int16(1), jnp.int16(0))
                for g in range(32):
                    accs[g % N_CNT_ACC] = accs[g % N_CNT_ACC] + ones[g]
            acc = (accs[0] + accs[1]) + (accs[2] + accs[3])
            a32 = pltpu.bitcast(acc, jnp.int32)
            return jnp.sum((a32 & 0xFFFF) + lax.shift_right_logical(a32, 16), axis=0, keepdims=True)

        def search16(ref, need):
            def bit_body(i, carry):
                t, above = carry
                cand = t + lax.shift_left(jnp.int32(1), 15 - i)
                cnt = count16(ref, cand)
                take = cnt >= need
                return jnp.where(take, cand, t), jnp.where(take, above, cnt)
            return lax.fori_loop(0, 16, bit_body, (jnp.full((1, LANES), INT16_MIN, jnp.int32),
                                                   jnp.zeros((1, LANES), jnp.int32)))

        t_hi, above_hi = search16(hi_ref, topk)

        t_hi32 = (t_hi & 0xFFFF) | lax.shift_left(t_hi, 16)
        t_hi16 = pltpu.bitcast(jnp.broadcast_to(t_hi32, (SUBLANES, LANES)), jnp.int16)

        def lo_prep(j, carry):
            r0 = pl.multiple_of(j * Q_BLOCK, Q_BLOCK)
            h16 = pltpu.bitcast(hi_ref[pl.ds(r0, Q_BLOCK), :], jnp.int16).reshape(16, 16, LANES)
            l16 = pltpu.bitcast(lo_ref[pl.ds(r0, Q_BLOCK), :], jnp.int16).reshape(16, 16, LANES)
            kept = jnp.where(h16 == t_hi16[None], l16, jnp.int16(INT16_MIN)).reshape(2 * Q_BLOCK, LANES)
            lo_ref[pl.ds(r0, Q_BLOCK), :] = pltpu.bitcast(kept, jnp.int32)
            return carry

        lax.fori_loop(0, 2 * n_steps, lo_prep, 0)
        t_lo, _ = search16(lo_ref, topk - above_hi)
        thr_ref[...] = lax.shift_left(t_hi, 16) | (t_lo - INT16_MIN)

    for n_steps in range(1, hi_ref.shape[0] // ATT_CHUNK + 1):
        pl.when(nquad == n_steps)(functools.partial(threshold, n_steps))

    thr = jnp.maximum(thr_ref[...], KEY_NEG_INF + 1)

    def count32(pred):
        def body(j, acc):
            r0 = pl.multiple_of(j * Q_BLOCK, Q_BLOCK)
            idx = r0 + row_iota
            return acc + jnp.where(pred(keys_ref[pl.ds(r0, Q_BLOCK), :], idx), 1, 0)
        acc = lax.fori_loop(0, nk2, body, jnp.zeros((Q_BLOCK, LANES), jnp.int32))
        return jnp.sum(acc, axis=0, keepdims=True)

    def build_bias():
        def body(j, acc):
            r0 = pl.multiple_of(j * Q_BLOCK, Q_BLOCK)
            sel = keys_ref[pl.ds(r0, Q_BLOCK), :] >= thr
            kvb_ref[pl.ds(r0, Q_BLOCK), KV_RANK:2 * KV_RANK] = jnp.where(sel, 0.0, MASK_NEG).astype(jnp.bfloat16)
            return acc + jnp.where(sel, 1, 0)
        acc = lax.fori_loop(0, nk2, body, jnp.zeros((Q_BLOCK, LANES), jnp.int32))
        return jnp.sum(acc, axis=0, keepdims=True)

    n_sel = build_bias()

    @pl.when(jnp.max(n_sel) > topk)
    def _():
        need = topk - count32(lambda k, idx: k > thr)

        def jbit(i, jmax):
            cand = jmax + lax.shift_left(jnp.int32(1), idx_bits - 1 - i)
            f = count32(lambda k, idx: jnp.where(k == thr, idx, cand) < cand)
            return jnp.where(f < need, cand, jmax)

        jmax = lax.fori_loop(0, idx_bits, jbit, jnp.zeros((1, LANES), jnp.int32))

        def drop(j, carry):
            r0 = pl.multiple_of(j * Q_BLOCK, Q_BLOCK)
            k = keys_ref[pl.ds(r0, Q_BLOCK), :]
            tie_idx = jnp.where(k == thr, r0 + row_iota, 0)
            old = kvb_ref[pl.ds(r0, Q_BLOCK), KV_RANK:2 * KV_RANK].astype(jnp.float32)
            kvb_ref[pl.ds(r0, Q_BLOCK), KV_RANK:2 * KV_RANK] = jnp.where(
                tie_idx > jmax, MASK_NEG, old).astype(jnp.bfloat16)
            return carry

        lax.fori_loop(0, nk2, drop, 0)

    def qk_dot(c, lg_ref, mx_ref):
        r0 = pl.multiple_of(c * ATT_CHUNK, ATT_CHUNK)
        lg = _dot(kvb_ref[pl.ds(r0, ATT_CHUNK), :], qaug_ref[...])
        lg_ref[...] = lg
        mx_ref[...] = jnp.max(lg.reshape(ATT_CHUNK // SUBLANES, SUBLANES, N_HEADS * LANES), axis=0)

    def pv_update(c, p_ref, corr_ref):
        pv = _dot(ckvt_ref[0, c], p_ref[...])
        for hd in range(N_HEADS):
            lanes = slice(hd * LANES, (hd + 1) * LANES)
            ot_ref[:, lanes] = ot_ref[:, lanes] * corr_ref[hd:hd + 1, :] + pv[:, lanes]

    def softmax_chunk(lg_ref, mx_ref, p_ref, corr_ref):
        for hd in range(N_HEADS):
            lanes = slice(hd * LANES, (hd + 1) * LANES)
            m_old = m_ref[hd:hd + 1, :]
            m_new = jnp.maximum(m_old, jnp.max(mx_ref[:, lanes], axis=0, keepdims=True))
            p_ref[:, lanes] = jnp.exp2(lg_ref[:, lanes] - m_new).astype(jnp.bfloat16)
            corr_ref[hd:hd + 1, :] = jnp.exp2(m_old - m_new)
            m_ref[hd:hd + 1, :] = m_new

    m_ref[...] = jnp.full(m_ref.shape, -jnp.inf, jnp.float32)
    ot_ref[...] = jnp.zeros(ot_ref.shape, jnp.float32)
    pb_ref[...] = jnp.zeros(pb_ref.shape, jnp.bfloat16)
    corrb_ref[...] = jnp.ones(corrb_ref.shape, jnp.float32)
    qk_dot(0, lga_ref, mxa_ref)
    last = 2 * nquad - 1

    def att_pair(i, carry):
        c0 = 2 * i
        qk_dot(c0 + 1, lgb_ref, mxb_ref)
        pv_update(jnp.maximum(c0 - 1, 0), pb_ref, corrb_ref)
        softmax_chunk(lga_ref, mxa_ref, pa_ref, corra_ref)
        qk_dot(jnp.minimum(c0 + 2, last), lga_ref, mxa_ref)
        pv_update(c0, pa_ref, corra_ref)
        softmax_chunk(lgb_ref, mxb_ref, pb_ref, corrb_ref)
        return carry

    lax.fori_loop(0, nquad, att_pair, 0)
    pv_update(last, pb_ref, corrb_ref)

    outs = []
    for hd in range(N_HEADS):
        lanes = slice(hd * LANES, (hd + 1) * LANES)
        o = (ot_ref[0:KV_RANK, lanes] / ot_ref[KV_RANK:KV_RANK + 1, lanes]).astype(jnp.bfloat16)
        outs.append(_dot(wuvt_ref[hd], o))
    att_t = jnp.concatenate(outs, axis=0)
    att_ref[0] = att_t.T.astype(jnp.bfloat16)


def _post_kernel(x_ref, att_ref, sig_ref, mconv_ref, p_ref, lng_ref, lnb_ref, wbra_ref, wo_ref,
                 ln1g_ref, ln1b_ref, wup_ref, cw_ref, wdown_ref, wpg_ref, bpg_ref, wple_ref,
                 ln2g_ref, ln2b_ref,
                 out_ref,
                 prea_ref, preb_ref, carry_ref, h1b_ref, act_ref, acc_ref):
    tm = x_ref.shape[1]
    n_pairs = N_FF_CHUNKS // 2

    @pl.when(pl.program_id(1) == 0)
    def _():
        carry_ref[...] = jnp.zeros(carry_ref.shape, jnp.float32)

    h = _layer_norm(x_ref[0], lng_ref[...], lnb_ref[...])
    merged = (sig_ref[0].astype(jnp.float32) * _dot(att_ref[0], wbra_ref[...])
              + mconv_ref[0].astype(jnp.float32))
    y = _dot(merged.astype(jnp.bfloat16), wo_ref[...])
    h1 = _layer_norm(ALPHA * h + y, ln1g_ref[...], ln1b_ref[...])
    h1b_ref[...] = h1.astype(jnp.bfloat16)
    ple = (jax.nn.sigmoid(_dot(h1b_ref[...], wpg_ref[...]) + bpg_ref[...])
           * _dot(p_ref[0, 0].astype(jnp.bfloat16), wple_ref[...]))

    def up_dot(j):
        return _dot(h1b_ref[...], wup_ref[j])

    def conv_act(pre_ref, j):
        pre_ref[0:SUBLANES, :] = carry_ref[j]
        cw = cw_ref[j]
        pre = pre_ref[SUBLANES:SUBLANES + tm, :]
        conv = (cw[0:1, :] * pre_ref[SUBLANES - 2:SUBLANES - 2 + tm, :]
                + cw[1:2, :] * pre_ref[SUBLANES - 1:SUBLANES - 1 + tm, :]
                + cw[2:3, :] * pre + cw[3:4, :])
        carry_ref[j] = pre[tm - SUBLANES:tm, :]
        g = conv[:, :FF_CHUNK]
        u = conv[:, FF_CHUNK:]
        return (g * jax.nn.sigmoid(g) * u).astype(jnp.bfloat16)

    prea_ref[SUBLANES:SUBLANES + tm, :] = up_dot(0)
    act_ref[...] = jnp.zeros(act_ref.shape, jnp.bfloat16)
    acc_ref[...] = jnp.zeros(acc_ref.shape, jnp.float32)

    def ff_pair(i, carry):
        j0 = 2 * i
        r0 = pl.multiple_of(jnp.maximum(i - 1, 0) * (2 * FF_CHUNK), 2 * FF_CHUNK)
        acc_ref[...] += _dot(act_ref[...], wdown_ref[pl.ds(r0, 2 * FF_CHUNK), :])
        preb_ref[SUBLANES:SUBLANES + tm, :] = up_dot(j0 + 1)
        act_ref[:, 0:FF_CHUNK] = conv_act(prea_ref, j0)
        prea_ref[SUBLANES:SUBLANES + tm, :] = up_dot(jnp.minimum(j0 + 2, N_FF_CHUNKS - 1))
        act_ref[:, FF_CHUNK:2 * FF_CHUNK] = conv_act(preb_ref, j0 + 1)
        return carry

    lax.fori_loop(0, n_pairs, ff_pair, 0)
    ffn = acc_ref[...] + _dot(act_ref[...], wdown_ref[(n_pairs - 1) * 2 * FF_CHUNK:n_pairs * 2 * FF_CHUNK, :])
    if N_FF_CHUNKS % 2:
        ffn = ffn + _dot(conv_act(prea_ref, N_FF_CHUNKS - 1), wdown_ref[n_pairs * 2 * FF_CHUNK:D_FF, :])

    out_ref[0] = _layer_norm(ALPHA * h1 + ffn + ple, ln2g_ref[...], ln2b_ref[...])


def _const_spec(shape):
    nd = len(shape)
    return pl.BlockSpec(shape, lambda *_: (0,) * nd, pipeline_mode=pl.Buffered(1))


def kernel(x, p, ln_emb_g, ln_emb_b, w_in, b_gate, kv_norm_g, w_uk, w_uv, k_idx_ln_g, k_idx_ln_b,
           mix_conv_w, mix_conv_b, w_br_att, w_br_conv, w_o, ln1_g, ln1_b, w_ffn_up, ffn_conv_w,
           ffn_conv_b, w_ffn_down, w_ple_gate, b_ple_gate, w_ple, ln2_g, ln2_b):
    bsz, seq, _ = x.shape
    assert seq % TM_PRE == 0 and seq % TM_POST == 0 and TM_PRE % (2 * ATT_CHUNK) == 0
    nb = seq // Q_BLOCK
    topk = min(TOPK_MAX, seq // 4)
    idx_bits = max(1, (seq - 1).bit_length())
    bf16, f32 = jnp.bfloat16, jnp.float32
    i = 0

    o_qatt = 0
    o_ckv = o_qatt + ATT_WIDTH
    o_qidx = o_ckv + KV_RANK
    o_kidx = o_qidx + N_IDX_HEADS * IDX_DIM
    o_widx = o_kidx + IDX_DIM
    o_cvb = o_widx + N_IDX_HEADS
    win = w_in[i]
    pad64 = jnp.zeros((D_MODEL, LANES - IDX_DIM), f32)
    w_tok = jnp.concatenate(
        [win[:, o_ckv:o_ckv + KV_RANK], win[:, o_kidx:o_kidx + IDX_DIM], pad64, win[:, o_cvb:]],
        axis=1).astype(bf16)
    w_qt = jnp.concatenate([win[:, o_qatt:o_qatt + ATT_WIDTH],
                            win[:, o_qidx:o_qidx + N_IDX_HEADS * IDX_DIM]], axis=1).T.astype(bf16)
    w_wt = jnp.concatenate([win[:, o_widx:o_widx + N_IDX_HEADS].T,
                            jnp.zeros((16 - N_IDX_HEADS, D_MODEL), f32)], axis=0).astype(bf16)
    row = lambda v: v.reshape(1, -1).astype(f32)
    kig = jnp.concatenate([k_idx_ln_g[i], jnp.zeros((LANES - IDX_DIM,), f32)]).reshape(1, LANES)
    kib = jnp.concatenate([k_idx_ln_b[i], jnp.zeros((LANES - IDX_DIM,), f32)]).reshape(1, LANES)
    cw_mix = jnp.concatenate([mix_conv_w[i], mix_conv_b[i][None, :],
                              jnp.zeros((SUBLANES - 4, D_CONV), f32)], axis=0)

    n_pre = seq // TM_PRE
    sub_pre = TM_PRE // Q_BLOCK
    tok3 = lambda w: pl.BlockSpec((1, TM_PRE, w), lambda b, t: (b, t, 0))
    blk4 = lambda r, w: pl.BlockSpec((1, sub_pre, r, w), lambda b, t: (b, t, 0, 0))
    pre_out_shapes = (
        jax.ShapeDtypeStruct((bsz, seq, KV_RANK), bf16),
        jax.ShapeDtypeStruct((bsz, seq // ATT_CHUNK, OT_ROWS, ATT_CHUNK), bf16),
        jax.ShapeDtypeStruct((bsz, seq, IDX_DIM), bf16),
        jax.ShapeDtypeStruct((bsz, nb, KV_RANK, N_HEADS * LANES), bf16),
        jax.ShapeDtypeStruct((bsz, nb, IDX_DIM, N_IDX_HEADS * LANES), bf16),
        jax.ShapeDtypeStruct((bsz, nb, N_IDX_HEADS, Q_BLOCK), f32),
        jax.ShapeDtypeStruct((bsz, seq, D_MODEL), bf16),
        jax.ShapeDtypeStruct((bsz, seq, D_MODEL), bf16),
    )
    ckv, ckvt, kidx, qlatt, qidxt, wt, mconv, sig = pl.pallas_call(
        _pre_kernel,
        grid=(bsz, n_pre),
        in_specs=[
            tok3(D_MODEL),
            _const_spec((1, D_MODEL)), _const_spec((1, D_MODEL)),
            _const_spec((D_MODEL, _TOK_WIDTH)), _const_spec((2 * ATT_WIDTH, D_MODEL)),
            _const_spec((16, D_MODEL)),
            _const_spec((1, KV_RANK)), _const_spec((1, LANES)), _const_spec((1, LANES)),
            _const_spec((N_HEADS, KV_RANK, HEAD_DIM)),
            _const_spec((SUBLANES, D_CONV)), _const_spec((D_CONV, D_MODEL)), _const_spec((2, D_MODEL)),
        ],
        out_specs=(
            tok3(KV_RANK),
            pl.BlockSpec((1, TM_PRE // ATT_CHUNK, OT_ROWS, ATT_CHUNK), lambda b, t: (b, t, 0, 0)),
            tok3(IDX_DIM),
            blk4(KV_RANK, N_HEADS * LANES), blk4(IDX_DIM, N_IDX_HEADS * LANES),
            blk4(N_IDX_HEADS, Q_BLOCK), tok3(D_MODEL), tok3(D_MODEL),
        ),
        out_shape=pre_out_shapes,
        scratch_shapes=[pltpu.VMEM((TM_PRE + SUBLANES, D_CONV), f32)],
        compiler_params=pltpu.CompilerParams(
            dimension_semantics=("arbitrary", "arbitrary"), vmem_limit_bytes=VMEM_LIMIT),
    )(x, row(ln_emb_g), row(ln_emb_b), w_tok, w_qt, w_wt, row(kv_norm_g[i]), kig, kib,
      w_uk[i].astype(bf16), cw_mix, w_br_conv[i].astype(bf16), b_gate[i].astype(f32))

    w_uvt = jnp.swapaxes(w_uv[i], 1, 2).astype(bf16)
    att = pl.pallas_call(
        functools.partial(_attn_kernel, topk=topk, idx_bits=idx_bits),
        grid=(bsz, nb),
        in_specs=[
            pl.BlockSpec((1, 1, KV_RANK, N_HEADS * LANES), lambda b, q: (b, q, 0, 0)),
            pl.BlockSpec((1, 1, IDX_DIM, N_IDX_HEADS * LANES), lambda b, q: (b, q, 0, 0)),
            pl.BlockSpec((1, 1, N_IDX_HEADS, Q_BLOCK), lambda b, q: (b, q, 0, 0)),
            pl.BlockSpec((1, seq, KV_RANK), lambda b, q: (b, 0, 0)),
            pl.BlockSpec((1, seq // ATT_CHUNK, OT_ROWS, ATT_CHUNK), lambda b, q: (b, 0, 0, 0)),
            pl.BlockSpec((1, seq, IDX_DIM), lambda b, q: (b, 0, 0)),
            _const_spec((N_HEADS, HEAD_DIM, KV_RANK)),
        ],
        out_specs=pl.BlockSpec((1, Q_BLOCK, ATT_WIDTH), lambda b, q: (b, q, 0)),
        out_shape=jax.ShapeDtypeStruct((bsz, seq, ATT_WIDTH), bf16),
        scratch_shapes=[
            pltpu.VMEM((seq, LANES), jnp.int32),
            pltpu.VMEM((seq // 2, LANES), jnp.int32),
            pltpu.VMEM((seq // 2, LANES), jnp.int32),
            pltpu.VMEM((seq, 2 * KV_RANK), bf16),
            pltpu.VMEM((2 * KV_RANK, N_HEADS * LANES), bf16),
            pltpu.VMEM((Q_BLOCK, N_IDX_HEADS * LANES), f32),
            pltpu.VMEM((Q_BLOCK, N_IDX_HEADS * LANES), f32),
            pltpu.VMEM((ATT_CHUNK, N_HEADS * LANES), f32),
            pltpu.VMEM((ATT_CHUNK, N_HEADS * LANES), f32),
            pltpu.VMEM((ATT_CHUNK, N_HEADS * LANES), bf16),
            pltpu.VMEM((ATT_CHUNK, N_HEADS * LANES), bf16),
            pltpu.VMEM((N_HEADS, LANES), f32),
            pltpu.VMEM((N_HEADS, LANES), f32),
            pltpu.VMEM((SUBLANES, N_HEADS * LANES), f32),
            pltpu.VMEM((SUBLANES, N_HEADS * LANES), f32),
            pltpu.VMEM((OT_ROWS, N_HEADS * LANES), f32),
            pltpu.VMEM((N_HEADS, LANES), f32),
            pltpu.VMEM((1, LANES), jnp.int32),
        ],
        compiler_params=pltpu.CompilerParams(
            dimension_semantics=("arbitrary", "arbitrary"), vmem_limit_bytes=VMEM_LIMIT),
    )(qlatt, qidxt, wt, ckv, ckvt, kidx, w_uvt)

    wup = w_ffn_up[i]
    wup_c = jnp.concatenate(
        [wup[:, :D_FF].reshape(D_MODEL, N_FF_CHUNKS, FF_CHUNK),
         wup[:, D_FF:].reshape(D_MODEL, N_FF_CHUNKS, FF_CHUNK)], axis=2)
    wup_c = jnp.transpose(wup_c, (1, 0, 2)).astype(bf16)
    cwf = jnp.concatenate([ffn_conv_w[i], ffn_conv_b[i][None, :],
                           jnp.zeros((SUBLANES - 4, 2 * D_FF), f32)], axis=0)
    cwf_c = jnp.concatenate(
        [cwf[:, :D_FF].reshape(SUBLANES, N_FF_CHUNKS, FF_CHUNK),
         cwf[:, D_FF:].reshape(SUBLANES, N_FF_CHUNKS, FF_CHUNK)], axis=2)
    cwf_c = jnp.transpose(cwf_c, (1, 0, 2))
    wdown_c = w_ffn_down[i].astype(bf16)

    n_post = seq // TM_POST
    tokp = lambda w: pl.BlockSpec((1, TM_POST, w), lambda b, t: (b, t, 0))
    out = pl.pallas_call(
        _post_kernel,
        grid=(bsz, n_post),
        in_specs=[
            tokp(D_MODEL), tokp(ATT_WIDTH), tokp(D_MODEL), tokp(D_MODEL),
            pl.BlockSpec((1, 1, TM_POST, PLE_DIM), lambda b, t: (0, b, t, 0)),
            _const_spec((1, D_MODEL)), _const_spec((1, D_MODEL)),
            _const_spec((ATT_WIDTH, D_MODEL)), _const_spec((D_MODEL, D_MODEL)),
            _const_spec((1, D_MODEL)), _const_spec((1, D_MODEL)),
            _const_spec((N_FF_CHUNKS, D_MODEL, 2 * FF_CHUNK)),
            _const_spec((N_FF_CHUNKS, SUBLANES, 2 * FF_CHUNK)),
            _const_spec((D_FF, D_MODEL)),
            _const_spec((D_MODEL, D_MODEL)), _const_spec((1, D_MODEL)),
            _const_spec((PLE_DIM, D_MODEL)),
            _const_spec((1, D_MODEL)), _const_spec((1, D_MODEL)),
        ],
        out_specs=tokp(D_MODEL),
        out_shape=jax.ShapeDtypeStruct((bsz, seq, D_MODEL), x.dtype),
        scratch_shapes=[
            pltpu.VMEM((TM_POST + SUBLANES, 2 * FF_CHUNK), f32),
            pltpu.VMEM((TM_POST + SUBLANES, 2 * FF_CHUNK), f32),
            pltpu.VMEM((N_FF_CHUNKS, SUBLANES, 2 * FF_CHUNK), f32),
            pltpu.VMEM((TM_POST, D_MODEL), bf16),
            pltpu.VMEM((TM_POST, 2 * FF_CHUNK), bf16),
            pltpu.VMEM((TM_POST, D_MODEL), f32),
        ],
        compiler_params=pltpu.CompilerParams(
            dimension_semantics=("arbitrary", "arbitrary"), vmem_limit_bytes=VMEM_LIMIT),
    )(x, att, sig, mconv, p, row(ln_emb_g), row(ln_emb_b), w_br_att[i].astype(bf16),
      w_o[i].astype(bf16), row(ln1_g[i]), row(ln1_b[i]), wup_c, cwf_c, wdown_c,
      w_ple_gate[i].astype(bf16), row(b_ple_gate[i]), w_ple[i].astype(bf16),
      row(ln2_g[i]), row(ln2_b[i]))
    return out
```

```python
import functools

import jax
import jax.numpy as jnp
from jax import lax
from jax.experimental import pallas as pl
from jax.experimental.pallas import tpu as pltpu

D_MODEL = 1024
PLE_DIM = 256
N_HEADS = 8
HEAD_DIM = 64
ATT_WIDTH = N_HEADS * HEAD_DIM
KV_RANK = 128
N_IDX_HEADS = 8
IDX_DIM = 64
TOPK_MAX = 256
Q_BLOCK = 128
D_CONV = 512
D_FF = 2816
LN_EPS = 1e-5
DEPTH = 1
ALPHA = (2.0 * DEPTH) ** 0.25
ATT_SCALE = HEAD_DIM ** -0.5
IDX_SCALE = IDX_DIM ** -0.5

LANES = 128
SUBLANES = 8
FF_CHUNK = 256
N_FF_CHUNKS = D_FF // FF_CHUNK
TM_PRE = 512
TM_POST = 256
ATT_CHUNK = 256
N_CNT_ACC = 4
VMEM_LIMIT = 56 * 1024 * 1024

INT_MIN = -(2 ** 31)
INT16_MIN = -(2 ** 15)
HI_MASK = -(2 ** 16)
SIGN16_X2 = -2147450880
LOG2E = 1.4426950408889634
OT_ROWS = KV_RANK + 16
KEY_NEG_INF = -2139095041
MASK_NEG = -(2.0 ** 126)

_TOK_CKV = 0
_TOK_KIDX = 128
_TOK_CVB = 256
_TOK_CVC = _TOK_CVB + D_CONV
_TOK_CVX = _TOK_CVC + D_CONV
_TOK_GATT = _TOK_CVX + D_CONV
_TOK_GCONV = _TOK_GATT + D_MODEL
_TOK_WIDTH = _TOK_GCONV + D_MODEL


def _layer_norm(x, g, b):
    mu = jnp.mean(x, axis=-1, keepdims=True)
    d = x - mu
    var = jnp.mean(d * d, axis=-1, keepdims=True)
    return d * lax.rsqrt(var + LN_EPS) * g + b


def _dot(a, b):
    return jnp.dot(a, b, preferred_element_type=jnp.float32)


def _dot_nt(a, b):
    return lax.dot_general(a, b, (((1,), (1,)), ((), ())), preferred_element_type=jnp.float32)


def _pre_kernel(x_ref, lng_ref, lnb_ref, wtok_ref, wqt_ref, wwt_ref, kvg_ref, kig_ref, kib_ref,
                wuk_ref, cw_ref, wbrc_ref, bg_ref,
                ckv_ref, ckvt_ref, kidx_ref, qlatt_ref, qidxt_ref, wt_ref, mconv_ref, sig_ref,
                zs_ref):
    tm = x_ref.shape[1]
    nsub = tm // Q_BLOCK

    @pl.when(pl.program_id(1) == 0)
    def _():
        zs_ref[0:SUBLANES, :] = jnp.zeros((SUBLANES, D_CONV), jnp.float32)

    h = _layer_norm(x_ref[0], lng_ref[...], lnb_ref[...])
    hb = h.astype(jnp.bfloat16)
    proj = _dot(hb, wtok_ref[...])

    c = proj[:, _TOK_CKV:_TOK_CKV + KV_RANK]
    c = c * lax.rsqrt(jnp.mean(c * c, axis=-1, keepdims=True) + LN_EPS) * kvg_ref[...]
    ckv_ref[0] = c.astype(jnp.bfloat16)
    ones_row = (lax.broadcasted_iota(jnp.int32, (OT_ROWS - KV_RANK, ATT_CHUNK), 0) == 0)
    for j in range(tm // ATT_CHUNK):
        ckvt_ref[0, j, 0:KV_RANK, :] = c[j * ATT_CHUNK:(j + 1) * ATT_CHUNK, :].T.astype(jnp.bfloat16)
        ckvt_ref[0, j, KV_RANK:OT_ROWS, :] = jnp.where(ones_row, 1.0, 0.0).astype(jnp.bfloat16)

    k = proj[:, _TOK_KIDX:_TOK_KIDX + LANES]
    valid = lax.broadcasted_iota(jnp.int32, k.shape, 1) < IDX_DIM
    mu = jnp.sum(k, axis=-1, keepdims=True) * (1.0 / IDX_DIM)
    d = jnp.where(valid, k - mu, 0.0)
    var = jnp.sum(d * d, axis=-1, keepdims=True) * (1.0 / IDX_DIM)
    kn = d * lax.rsqrt(var + LN_EPS) * kig_ref[...] + kib_ref[...]
    kidx_ref[0] = kn[:, :IDX_DIM].astype(jnp.bfloat16)

    z = proj[:, _TOK_CVC:_TOK_CVC + D_CONV] * proj[:, _TOK_CVX:_TOK_CVX + D_CONV]
    zs_ref[SUBLANES:SUBLANES + tm, :] = z
    conv = (cw_ref[0:1, :] * zs_ref[SUBLANES - 2:SUBLANES - 2 + tm, :]
            + cw_ref[1:2, :] * zs_ref[SUBLANES - 1:SUBLANES - 1 + tm, :]
            + cw_ref[2:3, :] * z + cw_ref[3:4, :])
    zs_ref[0:SUBLANES, :] = z[tm - SUBLANES:tm, :]
    conv_y = proj[:, _TOK_CVB:_TOK_CVB + D_CONV] * conv
    g_conv = jax.nn.sigmoid(proj[:, _TOK_GCONV:_TOK_GCONV + D_MODEL] + bg_ref[1:2, :])
    mconv_ref[0] = (g_conv * _dot(conv_y.astype(jnp.bfloat16), wbrc_ref[...])).astype(jnp.bfloat16)
    sig_ref[0] = jax.nn.sigmoid(proj[:, _TOK_GATT:_TOK_GATT + D_MODEL] + bg_ref[0:1, :]).astype(jnp.bfloat16)

    projt = _dot_nt(wqt_ref[...], hb)
    wt = _dot_nt(wwt_ref[...], hb)[0:N_IDX_HEADS, :] * (N_IDX_HEADS ** -0.5) * IDX_SCALE
    for hd in range(N_HEADS):
        qa = projt[hd * HEAD_DIM:(hd + 1) * HEAD_DIM, :].astype(jnp.bfloat16)
        ql = (_dot(wuk_ref[hd], qa) * (ATT_SCALE * LOG2E)).astype(jnp.bfloat16)
        qi = projt[ATT_WIDTH + hd * IDX_DIM:ATT_WIDTH + (hd + 1) * IDX_DIM, :].astype(jnp.bfloat16)
        for j in range(nsub):
            qlatt_ref[0, j, :, hd * LANES:(hd + 1) * LANES] = ql[:, j * Q_BLOCK:(j + 1) * Q_BLOCK]
            qidxt_ref[0, j, :, hd * LANES:(hd + 1) * LANES] = qi[:, j * Q_BLOCK:(j + 1) * Q_BLOCK]
    for j in range(nsub):
        wt_ref[0, j] = wt[:, j * Q_BLOCK:(j + 1) * Q_BLOCK]


def _attn_kernel(qlatt_ref, qidxt_ref, wt_ref, ckv_ref, ckvt_ref, kidx_ref, wuvt_ref,
                 att_ref,
                 keys_ref, hi_ref, lo_ref, kvb_ref, qaug_ref, rawa_ref, rawb_ref, lga_ref, lgb_ref, pa_ref, pb_ref,
                 corra_ref, corrb_ref, mxa_ref, mxb_ref, ot_ref, m_ref, thr_ref, *, topk, idx_bits):
    blk = pl.program_id(1)
    npair = (blk + 2) // 2
    nk2 = 2 * npair
    q0 = blk * Q_BLOCK
    row_iota = lax.broadcasted_iota(jnp.int32, (Q_BLOCK, LANES), 0)
    qpos = q0 + lax.broadcasted_iota(jnp.int32, (Q_BLOCK, LANES), 1)

    def raw_dot(c):
        r0 = pl.multiple_of(c * Q_BLOCK, Q_BLOCK)
        return _dot(kidx_ref[0, pl.ds(r0, Q_BLOCK), :], qidxt_ref[0, 0])

    def score_keys(raw_ref, c):
        acc = jnp.zeros((Q_BLOCK, LANES), jnp.float32)
        for hd in range(N_IDX_HEADS):
            acc = acc + (jnp.maximum(raw_ref[:, hd * LANES:(hd + 1) * LANES], 0.0)
                         * wt_ref[0, 0, hd:hd + 1, :])
        r0 = pl.multiple_of(c * Q_BLOCK, Q_BLOCK)
        acc = jnp.where(r0 + row_iota <= qpos, acc, -jnp.inf)
        bits = lax.bitcast_convert_type(acc, jnp.int32)
        key = bits ^ ((bits >> 31) & 0x7FFFFFFF)
        keys_ref[pl.ds(r0, Q_BLOCK), :] = key
        return key

    @pl.when(blk == 0)
    def _():
        kvb_ref[:, 0:KV_RANK] = ckv_ref[0]
        ident = jnp.where(row_iota == lax.broadcasted_iota(jnp.int32, (Q_BLOCK, LANES), 1), 1.0, 0.0)
        for hd in range(N_HEADS):
            qaug_ref[KV_RANK:2 * KV_RANK, hd * LANES:(hd + 1) * LANES] = ident.astype(jnp.bfloat16)

    qaug_ref[0:KV_RANK, :] = qlatt_ref[0, 0]

    rawa_ref[...] = raw_dot(0)

    def score_pair(i, carry):
        rawb_ref[...] = raw_dot(2 * i + 1)
        ka = score_keys(rawa_ref, 2 * i)
        rawa_ref[...] = raw_dot(jnp.minimum(2 * i + 2, nk2 - 2))
        kb = score_keys(rawb_ref, 2 * i + 1)
        r0 = pl.multiple_of(i * Q_BLOCK, Q_BLOCK)
        hi_ref[pl.ds(r0, Q_BLOCK), :] = lax.shift_right_logical(ka, 16) | (kb & HI_MASK)
        lo_ref[pl.ds(r0, Q_BLOCK), :] = ((ka & 0xFFFF) | lax.shift_left(kb, 16)) ^ SIGN16_X2
        return carry

    lax.fori_loop(0, npair, score_pair, 0)

    def threshold(n_steps):
        def count16(ref, cand):
            c32 = (cand & 0xFFFF) | lax.shift_left(cand, 16)
            c16 = pltpu.bitcast(jnp.broadcast_to(c32, (SUBLANES, LANES)), jnp.int16)
            accs = [jnp.zeros((16, LANES), jnp.int16)] * N_CNT_ACC
            for j in range(n_steps):
                k = pltpu.bitcast(ref[j * Q_BLOCK:(j + 1) * Q_BLOCK, :], jnp.int16)
                ones = jnp.where(k.reshape(16, 16, LANES) >= c16[None], jnp.int16(1), jnp.int16(0))
                for g in range(16):
                    accs[g % N_CNT_ACC] = accs[g % N_CNT_ACC] + ones[g]
            acc = (accs[0] + accs[1]) + (accs[2] + accs[3])
            a32 = pltpu.bitcast(acc, jnp.int32)
            return jnp.sum((a32 & 0xFFFF) + lax.shift_right_logical(a32, 16), axis=0, keepdims=True)

        def search16(ref, need):
            def bit_body(i, carry):
                t, above = carry
                cand = t + lax.shift_left(jnp.int32(1), 15 - i)
                cnt = count16(ref, cand)
                take = cnt >= need
                return jnp.where(take, cand, t), jnp.where(take, above, cnt)
            return lax.fori_loop(0, 16, bit_body, (jnp.full((1, LANES), INT16_MIN, jnp.int32),
                                                   jnp.zeros((1, LANES), jnp.int32)))

        t_hi, above_hi = search16(hi_ref, topk)

        t_hi32 = (t_hi & 0xFFFF) | lax.shift_left(t_hi, 16)
        t_hi16 = pltpu.bitcast(jnp.broadcast_to(t_hi32, (SUBLANES, LANES)), jnp.int16)

        def lo_prep(j, carry):
            r0 = pl.multiple_of(j * Q_BLOCK, Q_BLOCK)
            h16 = pltpu.bitcast(hi_ref[pl.ds(r0, Q_BLOCK), :], jnp.int16).reshape(16, 16, LANES)
            l16 = pltpu.bitcast(lo_ref[pl.ds(r0, Q_BLOCK), :], jnp.int16).reshape(16, 16, LANES)
            kept = jnp.where(h16 == t_hi16[None], l16, jnp.int16(INT16_MIN)).reshape(2 * Q_BLOCK, LANES)
            lo_ref[pl.ds(r0, Q_BLOCK), :] = pltpu.bitcast(kept, jnp.int32)
            return carry

        lax.fori_loop(0, n_steps, lo_prep, 0)
        t_lo, _ = search16(lo_ref, topk - above_hi)
        thr_ref[...] = lax.shift_left(t_hi, 16) | (t_lo - INT16_MIN)

    for n_steps in range(1, hi_ref.shape[0] // Q_BLOCK + 1):
        pl.when(npair == n_steps)(functools.partial(threshold, n_steps))

    thr = jnp.maximum(thr_ref[...], KEY_NEG_INF + 1)

    def count32(pred):
        def body(j, acc):
            r0 = pl.multiple_of(j * Q_BLOCK, Q_BLOCK)
            idx = r0 + row_iota
            return acc + jnp.where(pred(keys_ref[pl.ds(r0, Q_BLOCK), :], idx), 1, 0)
        acc = lax.fori_loop(0, nk2, body, jnp.zeros((Q_BLOCK, LANES), jnp.int32))
        return jnp.sum(acc, axis=0, keepdims=True)

    def build_bias():
        def body(j, acc):
            r0 = pl.multiple_of(j * Q_BLOCK, Q_BLOCK)
            sel = keys_ref[pl.ds(r0, Q_BLOCK), :] >= thr
            kvb_ref[pl.ds(r0, Q_BLOCK), KV_RANK:2 * KV_RANK] = jnp.where(sel, 0.0, MASK_NEG).astype(jnp.bfloat16)
            return acc + jnp.where(sel, 1, 0)
        acc = lax.fori_loop(0, nk2, body, jnp.zeros((Q_BLOCK, LANES), jnp.int32))
        return jnp.sum(acc, axis=0, keepdims=True)

    n_sel = build_bias()

    @pl.when(jnp.max(n_sel) > topk)
    def _():
        need = topk - count32(lambda k, idx: k > thr)

        def jbit(i, jmax):
            cand = jmax + lax.shift_left(jnp.int32(1), idx_bits - 1 - i)
            f = count32(lambda k, idx: jnp.where(k == thr, idx, cand) < cand)
            return jnp.where(f < need, cand, jmax)

        jmax = lax.fori_loop(0, idx_bits, jbit, jnp.zeros((1, LANES), jnp.int32))

        def drop(j, carry):
            r0 = pl.multiple_of(j * Q_BLOCK, Q_BLOCK)
            k = keys_ref[pl.ds(r0, Q_BLOCK), :]
            tie_idx = jnp.where(k == thr, r0 + row_iota, 0)
            old = kvb_ref[pl.ds(r0, Q_BLOCK), KV_RANK:2 * KV_RANK].astype(jnp.float32)
            kvb_ref[pl.ds(r0, Q_BLOCK), KV_RANK:2 * KV_RANK] = jnp.where(
                tie_idx > jmax, MASK_NEG, old).astype(jnp.bfloat16)
            return carry

        lax.fori_loop(0, nk2, drop, 0)

    def qk_dot(c, lg_ref, mx_ref):
        r0 = pl.multiple_of(c * ATT_CHUNK, ATT_CHUNK)
        lg = _dot(kvb_ref[pl.ds(r0, ATT_CHUNK), :], qaug_ref[...])
        lg_ref[...] = lg
        mx_ref[...] = jnp.max(lg.reshape(ATT_CHUNK // SUBLANES, SUBLANES, N_HEADS * LANES), axis=0)

    def pv_update(c, p_ref, corr_ref):
        pv = _dot(ckvt_ref[0, c], p_ref[...])
        for hd in range(N_HEADS):
            lanes = slice(hd * LANES, (hd + 1) * LANES)
            ot_ref[:, lanes] = ot_ref[:, lanes] * corr_ref[hd:hd + 1, :] + pv[:, lanes]

    def softmax_chunk(lg_ref, mx_ref, p_ref, corr_ref):
        for hd in range(N_HEADS):
            lanes = slice(hd * LANES, (hd + 1) * LANES)
            m_old = m_ref[hd:hd + 1, :]
            m_new = jnp.maximum(m_old, jnp.max(mx_ref[:, lanes], axis=0, keepdims=True))
            p_ref[:, lanes] = jnp.exp2(lg_ref[:, lanes] - m_new).astype(jnp.bfloat16)
            corr_ref[hd:hd + 1, :] = jnp.exp2(m_old - m_new)
            m_ref[hd:hd + 1, :] = m_new

    m_ref[...] = jnp.full(m_ref.shape, -jnp.inf, jnp.float32)
    ot_ref[...] = jnp.zeros(ot_ref.shape, jnp.float32)
    pb_ref[...] = jnp.zeros(pb_ref.shape, jnp.bfloat16)
    corrb_ref[...] = jnp.ones(corrb_ref.shape, jnp.float32)
    qk_dot(0, lga_ref, mxa_ref)
    last = npair - 1
    n_full = npair // 2

    def att_pair(i, carry):
        c0 = 2 * i
        qk_dot(c0 + 1, lgb_ref, mxb_ref)
        pv_update(jnp.maximum(c0 - 1, 0), pb_ref, corrb_ref)
        softmax_chunk(lga_ref, mxa_ref, pa_ref, corra_ref)
        qk_dot(jnp.minimum(c0 + 2, last), lga_ref, mxa_ref)
        pv_update(c0, pa_ref, corra_ref)
        softmax_chunk(lgb_ref, mxb_ref, pb_ref, corrb_ref)
        return carry

    lax.fori_loop(0, n_full, att_pair, 0)
    pv_update(jnp.maximum(2 * n_full - 1, 0), pb_ref, corrb_ref)

    @pl.when(npair % 2 == 1)
    def _():
        softmax_chunk(lga_ref, mxa_ref, pa_ref, corra_ref)
        pv_update(last, pa_ref, corra_ref)

    outs = []
    for hd in range(N_HEADS):
        lanes = slice(hd * LANES, (hd + 1) * LANES)
        o = (ot_ref[0:KV_RANK, lanes] / ot_ref[KV_RANK:KV_RANK + 1, lanes]).astype(jnp.bfloat16)
        outs.append(_dot(wuvt_ref[hd], o))
    att_t = jnp.concatenate(outs, axis=0)
    att_ref[0] = att_t.T.astype(jnp.bfloat16)


def _post_kernel(x_ref, att_ref, sig_ref, mconv_ref, p_ref, lng_ref, lnb_ref, wbra_ref, wo_ref,
                 ln1g_ref, ln1b_ref, wup_ref, cw_ref, wdown_ref, wpg_ref, bpg_ref, wple_ref,
                 ln2g_ref, ln2b_ref,
                 out_ref,
                 prea_ref, preb_ref, carry_ref, h1b_ref, act_ref, acc_ref):
    tm = x_ref.shape[1]
    n_pairs = N_FF_CHUNKS // 2

    @pl.when(pl.program_id(1) == 0)
    def _():
        carry_ref[...] = jnp.zeros(carry_ref.shape, jnp.float32)

    h = _layer_norm(x_ref[0], lng_ref[...], lnb_ref[...])
    merged = (sig_ref[0].astype(jnp.float32) * _dot(att_ref[0], wbra_ref[...])
              + mconv_ref[0].astype(jnp.float32))
    y = _dot(merged.astype(jnp.bfloat16), wo_ref[...])
    h1 = _layer_norm(ALPHA * h + y, ln1g_ref[...], ln1b_ref[...])
    h1b_ref[...] = h1.astype(jnp.bfloat16)
    ple = (jax.nn.sigmoid(_dot(h1b_ref[...], wpg_ref[...]) + bpg_ref[...])
           * _dot(p_ref[0, 0].astype(jnp.bfloat16), wple_ref[...]))

    def up_dot(j):
        return _dot(h1b_ref[...], wup_ref[j])

    def conv_act(pre_ref, j):
        pre_ref[0:SUBLANES, :] = carry_ref[j]
        cw = cw_ref[j]
        pre = pre_ref[SUBLANES:SUBLANES + tm, :]
        conv = (cw[0:1, :] * pre_ref[SUBLANES - 2:SUBLANES - 2 + tm, :]
                + cw[1:2, :] * pre_ref[SUBLANES - 1:SUBLANES - 1 + tm, :]
                + cw[2:3, :] * pre + cw[3:4, :])
        carry_ref[j] = pre[tm - SUBLANES:tm, :]
        g = conv[:, :FF_CHUNK]
        u = conv[:, FF_CHUNK:]
        return (g * jax.nn.sigmoid(g) * u).astype(jnp.bfloat16)

    prea_ref[SUBLANES:SUBLANES + tm, :] = up_dot(0)
    act_ref[...] = jnp.zeros(act_ref.shape, jnp.bfloat16)
    acc_ref[...] = jnp.zeros(acc_ref.shape, jnp.float32)

    def ff_pair(i, carry):
        j0 = 2 * i
        r0 = pl.multiple_of(jnp.maximum(i - 1, 0) * (2 * FF_CHUNK), 2 * FF_CHUNK)
        acc_ref[...] += _dot(act_ref[...], wdown_ref[pl.ds(r0, 2 * FF_CHUNK), :])
        preb_ref[SUBLANES:SUBLANES + tm, :] = up_dot(j0 + 1)
        act_ref[:, 0:FF_CHUNK] = conv_act(prea_ref, j0)
        prea_ref[SUBLANES:SUBLANES + tm, :] = up_dot(jnp.minimum(j0 + 2, N_FF_CHUNKS - 1))
        act_ref[:, FF_CHUNK:2 * FF_CHUNK] = conv_act(preb_ref, j0 + 1)
        return carry

    lax.fori_loop(0, n_pairs, ff_pair, 0)
    ffn = acc_ref[...] + _dot(act_ref[...], wdown_ref[(n_pairs - 1) * 2 * FF_CHUNK:n_pairs * 2 * FF_CHUNK, :])
    if N_FF_CHUNKS % 2:
        ffn = ffn + _dot(conv_act(prea_ref, N_FF_CHUNKS - 1), wdown_ref[n_pairs * 2 * FF_CHUNK:D_FF, :])

    out_ref[0] = _layer_norm(ALPHA * h1 + ffn + ple, ln2g_ref[...], ln2b_ref[...])


def _const_spec(shape):
    nd = len(shape)
    return pl.BlockSpec(shape, lambda *_: (0,) * nd, pipeline_mode=pl.Buffered(1))


def kernel(x, p, ln_emb_g, ln_emb_b, w_in, b_gate, kv_norm_g, w_uk, w_uv, k_idx_ln_g, k_idx_ln_b,
           mix_conv_w, mix_conv_b, w_br_att, w_br_conv, w_o, ln1_g, ln1_b, w_ffn_up, ffn_conv_w,
           ffn_conv_b, w_ffn_down, w_ple_gate, b_ple_gate, w_ple, ln2_g, ln2_b):
    bsz, seq, _ = x.shape
    assert seq % TM_PRE == 0 and seq % TM_POST == 0 and TM_PRE % (2 * ATT_CHUNK) == 0
    nb = seq // Q_BLOCK
    topk = min(TOPK_MAX, seq // 4)
    idx_bits = max(1, (seq - 1).bit_length())
    bf16, f32 = jnp.bfloat16, jnp.float32
    i = 0

    o_qatt = 0
    o_ckv = o_qatt + ATT_WIDTH
    o_qidx = o_ckv + KV_RANK
    o_kidx = o_qidx + N_IDX_HEADS * IDX_DIM
    o_widx = o_kidx + IDX_DIM
    o_cvb = o_widx + N_IDX_HEADS
    win = w_in[i]
    pad64 = jnp.zeros((D_MODEL, LANES - IDX_DIM), f32)
    w_tok = jnp.concatenate(
        [win[:, o_ckv:o_ckv + KV_RANK], win[:, o_kidx:o_kidx + IDX_DIM], pad64, win[:, o_cvb:]],
        axis=1).astype(bf16)
    w_qt = jnp.concatenate([win[:, o_qatt:o_qatt + ATT_WIDTH],
                            win[:, o_qidx:o_qidx + N_IDX_HEADS * IDX_DIM]], axis=1).T.astype(bf16)
    w_wt = jnp.concatenate([win[:, o_widx:o_widx + N_IDX_HEADS].T,
                            jnp.zeros((16 - N_IDX_HEADS, D_MODEL), f32)], axis=0).astype(bf16)
    row = lambda v: v.reshape(1, -1).astype(f32)
    kig = jnp.concatenate([k_idx_ln_g[i], jnp.zeros((LANES - IDX_DIM,), f32)]).reshape(1, LANES)
    kib = jnp.concatenate([k_idx_ln_b[i], jnp.zeros((LANES - IDX_DIM,), f32)]).reshape(1, LANES)
    cw_mix = jnp.concatenate([mix_conv_w[i], mix_conv_b[i][None, :],
                              jnp.zeros((SUBLANES - 4, D_CONV), f32)], axis=0)

    n_pre = seq // TM_PRE
    sub_pre = TM_PRE // Q_BLOCK
    tok3 = lambda w: pl.BlockSpec((1, TM_PRE, w), lambda b, t: (b, t, 0))
    blk4 = lambda r, w: pl.BlockSpec((1, sub_pre, r, w), lambda b, t: (b, t, 0, 0))
    pre_out_shapes = (
        jax.ShapeDtypeStruct((bsz, seq, KV_RANK), bf16),
        jax.ShapeDtypeStruct((bsz, seq // ATT_CHUNK, OT_ROWS, ATT_CHUNK), bf16),
        jax.ShapeDtypeStruct((bsz, seq, IDX_DIM), bf16),
        jax.ShapeDtypeStruct((bsz, nb, KV_RANK, N_HEADS * LANES), bf16),
        jax.ShapeDtypeStruct((bsz, nb, IDX_DIM, N_IDX_HEADS * LANES), bf16),
        jax.ShapeDtypeStruct((bsz, nb, N_IDX_HEADS, Q_BLOCK), f32),
        jax.ShapeDtypeStruct((bsz, seq, D_MODEL), bf16),
        jax.ShapeDtypeStruct((bsz, seq, D_MODEL), bf16),
    )
    ckv, ckvt, kidx, qlatt, qidxt, wt, mconv, sig = pl.pallas_call(
        _pre_kernel,
        grid=(bsz, n_pre),
        in_specs=[
            tok3(D_MODEL),
            _const_spec((1, D_MODEL)), _const_spec((1, D_MODEL)),
            _const_spec((D_MODEL, _TOK_WIDTH)), _const_spec((2 * ATT_WIDTH, D_MODEL)),
            _const_spec((16, D_MODEL)),
            _const_spec((1, KV_RANK)), _const_spec((1, LANES)), _const_spec((1, LANES)),
            _const_spec((N_HEADS, KV_RANK, HEAD_DIM)),
            _const_spec((SUBLANES, D_CONV)), _const_spec((D_CONV, D_MODEL)), _const_spec((2, D_MODEL)),
        ],
        out_specs=(
            tok3(KV_RANK),
            pl.BlockSpec((1, TM_PRE // ATT_CHUNK, OT_ROWS, ATT_CHUNK), lambda b, t: (b, t, 0, 0)),
            tok3(IDX_DIM),
            blk4(KV_RANK, N_HEADS * LANES), blk4(IDX_DIM, N_IDX_HEADS * LANES),
            blk4(N_IDX_HEADS, Q_BLOCK), tok3(D_MODEL), tok3(D_MODEL),
        ),
        out_shape=pre_out_shapes,
        scratch_shapes=[pltpu.VMEM((TM_PRE + SUBLANES, D_CONV), f32)],
        compiler_params=pltpu.CompilerParams(
            dimension_semantics=("arbitrary", "arbitrary"), vmem_limit_bytes=VMEM_LIMIT),
    )(x, row(ln_emb_g), row(ln_emb_b), w_tok, w_qt, w_wt, row(kv_norm_g[i]), kig, kib,
      w_uk[i].astype(bf16), cw_mix, w_br_conv[i].astype(bf16), b_gate[i].astype(f32))

    w_uvt = jnp.swapaxes(w_uv[i], 1, 2).astype(bf16)
    att = pl.pallas_call(
        functools.partial(_attn_kernel, topk=topk, idx_bits=idx_bits),
        grid=(bsz, nb),
        in_specs=[
            pl.BlockSpec((1, 1, KV_RANK, N_HEADS * LANES), lambda b, q: (b, q, 0, 0)),
            pl.BlockSpec((1, 1, IDX_DIM, N_IDX_HEADS * LANES), lambda b, q: (b, q, 0, 0)),
            pl.BlockSpec((1, 1, N_IDX_HEADS, Q_BLOCK), lambda b, q: (b, q, 0, 0)),
            pl.BlockSpec((1, seq, KV_RANK), lambda b, q: (b, 0, 0)),
            pl.BlockSpec((1, seq // ATT_CHUNK, OT_ROWS, ATT_CHUNK), lambda b, q: (b, 0, 0, 0)),
            pl.BlockSpec((1, seq, IDX_DIM), lambda b, q: (b, 0, 0)),
            _const_spec((N_HEADS, HEAD_DIM, KV_RANK)),
        ],
        out_specs=pl.BlockSpec((1, Q_BLOCK, ATT_WIDTH), lambda b, q: (b, q, 0)),
        out_shape=jax.ShapeDtypeStruct((bsz, seq, ATT_WIDTH), bf16),
        scratch_shapes=[
            pltpu.VMEM((seq, LANES), jnp.int32),
            pltpu.VMEM((seq // 2, LANES), jnp.int32),
            pltpu.VMEM((seq // 2, LANES), jnp.int32),
            pltpu.VMEM((seq, 2 * KV_RANK), bf16),
            pltpu.VMEM((2 * KV_RANK, N_HEADS * LANES), bf16),
            pltpu.VMEM((Q_BLOCK, N_IDX_HEADS * LANES), f32),
            pltpu.VMEM((Q_BLOCK, N_IDX_HEADS * LANES), f32),
            pltpu.VMEM((ATT_CHUNK, N_HEADS * LANES), f32),
            pltpu.VMEM((ATT_CHUNK, N_HEADS * LANES), f32),
            pltpu.VMEM((ATT_CHUNK, N_HEADS * LANES), bf16),
            pltpu.VMEM((ATT_CHUNK, N_HEADS * LANES), bf16),
            pltpu.VMEM((N_HEADS, LANES), f32),
            pltpu.VMEM((N_HEADS, LANES), f32),
            pltpu.VMEM((SUBLANES, N_HEADS * LANES), f32),
            pltpu.VMEM((SUBLANES, N_HEADS * LANES), f32),
            pltpu.VMEM((OT_ROWS, N_HEADS * LANES), f32),
            pltpu.VMEM((N_HEADS, LANES), f32),
            pltpu.VMEM((1, LANES), jnp.int32),
        ],
        compiler_params=pltpu.CompilerParams(
            dimension_semantics=("arbitrary", "arbitrary"), vmem_limit_bytes=VMEM_LIMIT),
    )(qlatt, qidxt, wt, ckv, ckvt, kidx, w_uvt)

    wup = w_ffn_up[i]
    wup_c = jnp.concatenate(
        [wup[:, :D_FF].reshape(D_MODEL, N_FF_CHUNKS, FF_CHUNK),
         wup[:, D_FF:].reshape(D_MODEL, N_FF_CHUNKS, FF_CHUNK)], axis=2)
    wup_c = jnp.transpose(wup_c, (1, 0, 2)).astype(bf16)
    cwf = jnp.concatenate([ffn_conv_w[i], ffn_conv_b[i][None, :],
                           jnp.zeros((SUBLANES - 4, 2 * D_FF), f32)], axis=0)
    cwf_c = jnp.concatenate(
        [cwf[:, :D_FF].reshape(SUBLANES, N_FF_CHUNKS, FF_CHUNK),
         cwf[:, D_FF:].reshape(SUBLANES, N_FF_CHUNKS, FF_CHUNK)], axis=2)
    cwf_c = jnp.transpose(cwf_c, (1, 0, 2))
    wdown_c = w_ffn_down[i].astype(bf16)

    n_post = seq // TM_POST
    tokp = lambda w: pl.BlockSpec((1, TM_POST, w), lambda b, t: (b, t, 0))
    out = pl.pallas_call(
        _post_kernel,
        grid=(bsz, n_post),
        in_specs=[
            tokp(D_MODEL), tokp(ATT_WIDTH), tokp(D_MODEL), tokp(D_MODEL),
            pl.BlockSpec((1, 1, TM_POST, PLE_DIM), lambda b, t: (0, b, t, 0)),
            _const_spec((1, D_MODEL)), _const_spec((1, D_MODEL)),
            _const_spec((ATT_WIDTH, D_MODEL)), _const_spec((D_MODEL, D_MODEL)),
            _const_spec((1, D_MODEL)), _const_spec((1, D_MODEL)),
            _const_spec((N_FF_CHUNKS, D_MODEL, 2 * FF_CHUNK)),
            _const_spec((N_FF_CHUNKS, SUBLANES, 2 * FF_CHUNK)),
            _const_spec((D_FF, D_MODEL)),
            _const_spec((D_MODEL, D_MODEL)), _const_spec((1, D_MODEL)),
            _const_spec((PLE_DIM, D_MODEL)),
            _const_spec((1, D_MODEL)), _const_spec((1, D_MODEL)),
        ],
        out_specs=tokp(D_MODEL),
        out_shape=jax.ShapeDtypeStruct((bsz, seq, D_MODEL), x.dtype),
        scratch_shapes=[
            pltpu.VMEM((TM_POST + SUBLANES, 2 * FF_CHUNK), f32),
            pltpu.VMEM((TM_POST + SUBLANES, 2 * FF_CHUNK), f32),
            pltpu.VMEM((N_FF_CHUNKS, SUBLANES, 2 * FF_CHUNK), f32),
            pltpu.VMEM((TM_POST, D_MODEL), bf16),
            pltpu.VMEM((TM_POST, 2 * FF_CHUNK), bf16),
            pltpu.VMEM((TM_POST, D_MODEL), f32),
        ],
        compiler_params=pltpu.CompilerParams(
            dimension_semantics=("arbitrary", "arbitrary"), vmem_limit_bytes=VMEM_LIMIT),
    )(x, att, sig, mconv, p, row(ln_emb_g), row(ln_emb_b), w_br_att[i].astype(bf16),
      w_o[i].astype(bf16), row(ln1_g[i]), row(ln1_b[i]), wup_c, cwf_c, wdown_c,
      w_ple_gate[i].astype(bf16), row(b_ple_gate[i]), w_ple[i].astype(bf16),
      row(ln2_g[i]), row(ln2_b[i]))
    return out
```

```python
import functools

import jax
import jax.numpy as jnp
from jax import lax
from jax.experimental import pallas as pl
from jax.experimental.pallas import tpu as pltpu

D_MODEL = 1024
PLE_DIM = 256
N_HEADS = 8
HEAD_DIM = 64
ATT_WIDTH = N_HEADS * HEAD_DIM
KV_RANK = 128
N_IDX_HEADS = 8
IDX_DIM = 64
TOPK_MAX = 256
Q_BLOCK = 128
D_CONV = 512
D_FF = 2816
LN_EPS = 1e-5
DEPTH = 1
ALPHA = (2.0 * DEPTH) ** 0.25
ATT_SCALE = HEAD_DIM ** -0.5
IDX_SCALE = IDX_DIM ** -0.5

LANES = 128
SUBLANES = 8
FF_CHUNK = 256
N_FF_CHUNKS = D_FF // FF_CHUNK
TM_PRE = 1024
TM_POST = 512
ATT_CHUNK = 256
N_CNT_ACC = 4
VMEM_LIMIT = 56 * 1024 * 1024

SUBLANES_16 = 16
HALF_BITS = 16
PACKED_GROUPS = 2 * Q_BLOCK // SUBLANES_16

INT_MIN = -(2 ** 31)
INT16_MIN = -(2 ** 15)
HALF_MASK = 2 ** HALF_BITS - 1
HI_MASK = -(2 ** HALF_BITS)
SIGN16_X2 = -2147450880
LOG2E = 1.4426950408889634
OT_ROWS = KV_RANK + SUBLANES_16
KEY_NEG_INF = -2139095041
MASK_NEG = -(2.0 ** 126)

_TOK_CKV = 0
_TOK_KIDX = 128
_TOK_CVB = 256
_TOK_CVC = _TOK_CVB + D_CONV
_TOK_CVX = _TOK_CVC + D_CONV
_TOK_GATT = _TOK_CVX + D_CONV
_TOK_GCONV = _TOK_GATT + D_MODEL
_TOK_WIDTH = _TOK_GCONV + D_MODEL


def _layer_norm(x, g, b):
    mu = jnp.mean(x, axis=-1, keepdims=True)
    d = x - mu
    var = jnp.mean(d * d, axis=-1, keepdims=True)
    return d * lax.rsqrt(var + LN_EPS) * g + b


def _dot(a, b):
    return jnp.dot(a, b, preferred_element_type=jnp.float32)


def _dot_nt(a, b):
    return lax.dot_general(a, b, (((1,), (1,)), ((), ())), preferred_element_type=jnp.float32)


def _pre_kernel(x_ref, lng_ref, lnb_ref, wtok_ref, wqt_ref, wwt_ref, kvg_ref, kig_ref, kib_ref,
                wuk_ref, cw_ref, wbrc_ref, bg_ref,
                ckv_ref, ckvt_ref, kidx_ref, qlatt_ref, qidxt_ref, wt_ref, mconv_ref, sig_ref,
                zs_ref):
    tm = x_ref.shape[1]
    nsub = tm // Q_BLOCK

    @pl.when(pl.program_id(1) == 0)
    def _():
        zs_ref[0:SUBLANES, :] = jnp.zeros((SUBLANES, D_CONV), jnp.float32)

    h = _layer_norm(x_ref[0], lng_ref[...], lnb_ref[...])
    hb = h.astype(jnp.bfloat16)
    proj = _dot(hb, wtok_ref[...])

    c = proj[:, _TOK_CKV:_TOK_CKV + KV_RANK]
    c = c * lax.rsqrt(jnp.mean(c * c, axis=-1, keepdims=True) + LN_EPS) * kvg_ref[...]
    ckv_ref[0] = c.astype(jnp.bfloat16)
    ones_row = (lax.broadcasted_iota(jnp.int32, (OT_ROWS - KV_RANK, ATT_CHUNK), 0) == 0)
    for j in range(tm // ATT_CHUNK):
        ckvt_ref[0, j, 0:KV_RANK, :] = c[j * ATT_CHUNK:(j + 1) * ATT_CHUNK, :].T.astype(jnp.bfloat16)
        ckvt_ref[0, j, KV_RANK:OT_ROWS, :] = jnp.where(ones_row, 1.0, 0.0).astype(jnp.bfloat16)

    k = proj[:, _TOK_KIDX:_TOK_KIDX + LANES]
    valid = lax.broadcasted_iota(jnp.int32, k.shape, 1) < IDX_DIM
    mu = jnp.sum(k, axis=-1, keepdims=True) * (1.0 / IDX_DIM)
    d = jnp.where(valid, k - mu, 0.0)
    var = jnp.sum(d * d, axis=-1, keepdims=True) * (1.0 / IDX_DIM)
    kn = d * lax.rsqrt(var + LN_EPS) * kig_ref[...] + kib_ref[...]
    kidx_ref[0] = kn[:, :IDX_DIM].astype(jnp.bfloat16)

    z = proj[:, _TOK_CVC:_TOK_CVC + D_CONV] * proj[:, _TOK_CVX:_TOK_CVX + D_CONV]
    zs_ref[SUBLANES:SUBLANES + tm, :] = z
    conv = (cw_ref[0:1, :] * zs_ref[SUBLANES - 2:SUBLANES - 2 + tm, :]
            + cw_ref[1:2, :] * zs_ref[SUBLANES - 1:SUBLANES - 1 + tm, :]
            + cw_ref[2:3, :] * z + cw_ref[3:4, :])
    zs_ref[0:SUBLANES, :] = z[tm - SUBLANES:tm, :]
    conv_y = proj[:, _TOK_CVB:_TOK_CVB + D_CONV] * conv
    g_conv = jax.nn.sigmoid(proj[:, _TOK_GCONV:_TOK_GCONV + D_MODEL] + bg_ref[1:2, :])
    mconv_ref[0] = (g_conv * _dot(conv_y.astype(jnp.bfloat16), wbrc_ref[...])).astype(jnp.bfloat16)
    sig_ref[0] = jax.nn.sigmoid(proj[:, _TOK_GATT:_TOK_GATT + D_MODEL] + bg_ref[0:1, :]).astype(jnp.bfloat16)

    projt = _dot_nt(wqt_ref[...], hb)
    wt = _dot_nt(wwt_ref[...], hb)[0:N_IDX_HEADS, :] * (N_IDX_HEADS ** -0.5) * IDX_SCALE
    for hd in range(N_HEADS):
        qa = projt[hd * HEAD_DIM:(hd + 1) * HEAD_DIM, :].astype(jnp.bfloat16)
        ql = (_dot(wuk_ref[hd], qa) * (ATT_SCALE * LOG2E)).astype(jnp.bfloat16)
        qi = projt[ATT_WIDTH + hd * IDX_DIM:ATT_WIDTH + (hd + 1) * IDX_DIM, :].astype(jnp.bfloat16)
        for j in range(nsub):
            qlatt_ref[0, j, :, hd * LANES:(hd + 1) * LANES] = ql[:, j * Q_BLOCK:(j + 1) * Q_BLOCK]
            qidxt_ref[0, j, :, hd * LANES:(hd + 1) * LANES] = qi[:, j * Q_BLOCK:(j + 1) * Q_BLOCK]
    for j in range(nsub):
        wt_ref[0, j] = wt[:, j * Q_BLOCK:(j + 1) * Q_BLOCK]


def _attn_kernel(qlatt_ref, qidxt_ref, wt_ref, ckv_ref, ckvt_ref, kidx_ref, wuvt_ref,
                 att_ref,
                 keys_ref, hi_ref, lo_ref, kvb_ref, qaug_ref, rawa_ref, rawb_ref, lga_ref, lgb_ref, pa_ref, pb_ref,
                 corra_ref, corrb_ref, mxa_ref, mxb_ref, ot_ref, m_ref, thr_ref, *, topk, idx_bits):
    blk = pl.program_id(1)
    npair = (blk + 2) // 2
    nk2 = 2 * npair
    q0 = blk * Q_BLOCK
    row_iota = lax.broadcasted_iota(jnp.int32, (Q_BLOCK, LANES), 0)
    qpos = q0 + lax.broadcasted_iota(jnp.int32, (Q_BLOCK, LANES), 1)

    def raw_dot(c):
        r0 = pl.multiple_of(c * Q_BLOCK, Q_BLOCK)
        return _dot(kidx_ref[0, pl.ds(r0, Q_BLOCK), :], qidxt_ref[0, 0])

    def score_keys(raw_ref, c):
        acc = jnp.zeros((Q_BLOCK, LANES), jnp.float32)
        for hd in range(N_IDX_HEADS):
            acc = acc + (jnp.maximum(raw_ref[:, hd * LANES:(hd + 1) * LANES], 0.0)
                         * wt_ref[0, 0, hd:hd + 1, :])
        r0 = pl.multiple_of(c * Q_BLOCK, Q_BLOCK)
        acc = jnp.where(r0 + row_iota <= qpos, acc, -jnp.inf)
        bits = lax.bitcast_convert_type(acc, jnp.int32)
        key = bits ^ ((bits >> 31) & 0x7FFFFFFF)
        keys_ref[pl.ds(r0, Q_BLOCK), :] = key
        return key

    @pl.when(blk == 0)
    def _():
        kvb_ref[:, 0:KV_RANK] = ckv_ref[0]
        ident = jnp.where(row_iota == lax.broadcasted_iota(jnp.int32, (Q_BLOCK, LANES), 1), 1.0, 0.0)
        for hd in range(N_HEADS):
            qaug_ref[KV_RANK:2 * KV_RANK, hd * LANES:(hd + 1) * LANES] = ident.astype(jnp.bfloat16)

    qaug_ref[0:KV_RANK, :] = qlatt_ref[0, 0]

    rawa_ref[...] = raw_dot(0)

    def score_pair(i, carry):
        rawb_ref[...] = raw_dot(2 * i + 1)
        ka = score_keys(rawa_ref, 2 * i)
        rawa_ref[...] = raw_dot(jnp.minimum(2 * i + 2, nk2 - 2))
        kb = score_keys(rawb_ref, 2 * i + 1)
        r0 = pl.multiple_of(i * Q_BLOCK, Q_BLOCK)
        hi_ref[pl.ds(r0, Q_BLOCK), :] = lax.shift_right_logical(ka, HALF_BITS) | (kb & HI_MASK)
        lo_ref[pl.ds(r0, Q_BLOCK), :] = ((ka & HALF_MASK) | lax.shift_left(kb, HALF_BITS)) ^ SIGN16_X2
        return carry

    lax.fori_loop(0, npair, score_pair, 0)

    def threshold(n_steps):
        def count16(ref, cand):
            c32 = (cand & HALF_MASK) | lax.shift_left(cand, HALF_BITS)
            c16 = pltpu.bitcast(jnp.broadcast_to(c32, (SUBLANES, LANES)), jnp.int16)
            accs = [jnp.zeros((SUBLANES_16, LANES), jnp.int16)] * N_CNT_ACC
            for j in range(n_steps):
                k = pltpu.bitcast(ref[j * Q_BLOCK:(j + 1) * Q_BLOCK, :], jnp.int16)
                ones = jnp.where(k.reshape(PACKED_GROUPS, SUBLANES_16, LANES) >= c16[None],
                                 jnp.int16(1), jnp.int16(0))
                for g in range(PACKED_GROUPS):
                    accs[g % N_CNT_ACC] = accs[g % N_CNT_ACC] + ones[g]
            acc = (accs[0] + accs[1]) + (accs[2] + accs[3])
            a32 = pltpu.bitcast(acc, jnp.int32)
            return jnp.sum((a32 & HALF_MASK) + lax.shift_right_logical(a32, HALF_BITS), axis=0, keepdims=True)

        def search16(ref, need):
            def bit_body(i, carry):
                t, above = carry
                cand = t + lax.shift_left(jnp.int32(1), HALF_BITS - 1 - i)
                cnt = count16(ref, cand)
                take = cnt >= need
                return jnp.where(take, cand, t), jnp.where(take, above, cnt)
            return lax.fori_loop(0, HALF_BITS, bit_body, (jnp.full((1, LANES), INT16_MIN, jnp.int32),
                                                   jnp.zeros((1, LANES), jnp.int32)))

        t_hi, above_hi = search16(hi_ref, topk)

        t_hi32 = (t_hi & HALF_MASK) | lax.shift_left(t_hi, HALF_BITS)
        t_hi16 = pltpu.bitcast(jnp.broadcast_to(t_hi32, (SUBLANES, LANES)), jnp.int16)

        def lo_prep(j, carry):
            r0 = pl.multiple_of(j * Q_BLOCK, Q_BLOCK)
            packed = (PACKED_GROUPS, SUBLANES_16, LANES)
            h16 = pltpu.bitcast(hi_ref[pl.ds(r0, Q_BLOCK), :], jnp.int16).reshape(packed)
            l16 = pltpu.bitcast(lo_ref[pl.ds(r0, Q_BLOCK), :], jnp.int16).reshape(packed)
            kept = jnp.where(h16 == t_hi16[None], l16, jnp.int16(INT16_MIN)).reshape(2 * Q_BLOCK, LANES)
            lo_ref[pl.ds(r0, Q_BLOCK), :] = pltpu.bitcast(kept, jnp.int32)
            return carry

        lax.fori_loop(0, n_steps, lo_prep, 0)
        t_lo, _ = search16(lo_ref, topk - above_hi)
        thr_ref[...] = lax.shift_left(t_hi, HALF_BITS) | (t_lo - INT16_MIN)

    for n_steps in range(1, hi_ref.shape[0] // Q_BLOCK + 1):
        pl.when(npair == n_steps)(functools.partial(threshold, n_steps))

    thr = jnp.maximum(thr_ref[...], KEY_NEG_INF + 1)

    def count32(pred):
        def body(j, acc):
            r0 = pl.multiple_of(j * Q_BLOCK, Q_BLOCK)
            idx = r0 + row_iota
            return acc + jnp.where(pred(keys_ref[pl.ds(r0, Q_BLOCK), :], idx), 1, 0)
        acc = lax.fori_loop(0, nk2, body, jnp.zeros((Q_BLOCK, LANES), jnp.int32))
        return jnp.sum(acc, axis=0, keepdims=True)

    def build_bias():
        def body(j, acc):
            r0 = pl.multiple_of(j * Q_BLOCK, Q_BLOCK)
            sel = keys_ref[pl.ds(r0, Q_BLOCK), :] >= thr
            kvb_ref[pl.ds(r0, Q_BLOCK), KV_RANK:2 * KV_RANK] = jnp.where(sel, 0.0, MASK_NEG).astype(jnp.bfloat16)
            return acc + jnp.where(sel, 1, 0)
        acc = lax.fori_loop(0, nk2, body, jnp.zeros((Q_BLOCK, LANES), jnp.int32))
        return jnp.sum(acc, axis=0, keepdims=True)

    n_sel = build_bias()

    @pl.when(jnp.max(n_sel) > topk)
    def _():
        need = topk - count32(lambda k, idx: k > thr)

        def jbit(i, jmax):
            cand = jmax + lax.shift_left(jnp.int32(1), idx_bits - 1 - i)
            f = count32(lambda k, idx: jnp.where(k == thr, idx, cand) < cand)
            return jnp.where(f < need, cand, jmax)

        jmax = lax.fori_loop(0, idx_bits, jbit, jnp.zeros((1, LANES), jnp.int32))

        def drop(j, carry):
            r0 = pl.multiple_of(j * Q_BLOCK, Q_BLOCK)
            k = keys_ref[pl.ds(r0, Q_BLOCK), :]
            tie_idx = jnp.where(k == thr, r0 + row_iota, 0)
            old = kvb_ref[pl.ds(r0, Q_BLOCK), KV_RANK:2 * KV_RANK].astype(jnp.float32)
            kvb_ref[pl.ds(r0, Q_BLOCK), KV_RANK:2 * KV_RANK] = jnp.where(
                tie_idx > jmax, MASK_NEG, old).astype(jnp.bfloat16)
            return carry

        lax.fori_loop(0, nk2, drop, 0)

    def qk_dot(c, lg_ref, mx_ref):
        r0 = pl.multiple_of(c * ATT_CHUNK, ATT_CHUNK)
        lg = _dot(kvb_ref[pl.ds(r0, ATT_CHUNK), :], qaug_ref[...])
        lg_ref[...] = lg
        mx_ref[...] = jnp.max(lg.reshape(ATT_CHUNK // SUBLANES, SUBLANES, N_HEADS * LANES), axis=0)

    def pv_update(c, p_ref, corr_ref):
        pv = _dot(ckvt_ref[0, c], p_ref[...])
        for hd in range(N_HEADS):
            lanes = slice(hd * LANES, (hd + 1) * LANES)
            ot_ref[:, lanes] = ot_ref[:, lanes] * corr_ref[hd:hd + 1, :] + pv[:, lanes]

    def softmax_chunk(lg_ref, mx_ref, p_ref, corr_ref):
        for hd in range(N_HEADS):
            lanes = slice(hd * LANES, (hd + 1) * LANES)
            m_old = m_ref[hd:hd + 1, :]
            m_new = jnp.maximum(m_old, jnp.max(mx_ref[:, lanes], axis=0, keepdims=True))
            p_ref[:, lanes] = jnp.exp2(lg_ref[:, lanes] - m_new).astype(jnp.bfloat16)
            corr_ref[hd:hd + 1, :] = jnp.exp2(m_old - m_new)
            m_ref[hd:hd + 1, :] = m_new

    m_ref[...] = jnp.full(m_ref.shape, -jnp.inf, jnp.float32)
    ot_ref[...] = jnp.zeros(ot_ref.shape, jnp.float32)
    qk_dot(0, lga_ref, mxa_ref)
    last = npair - 1
    n_full = npair // 2

    def att_pair(i, carry):
        c0 = 2 * i
        qk_dot(c0 + 1, lgb_ref, mxb_ref)
        softmax_chunk(lga_ref, mxa_ref, pa_ref, corra_ref)
        pv_update(c0, pa_ref, corra_ref)
        qk_dot(jnp.minimum(c0 + 2, last), lga_ref, mxa_ref)
        softmax_chunk(lgb_ref, mxb_ref, pb_ref, corrb_ref)
        pv_update(c0 + 1, pb_ref, corrb_ref)
        return carry

    lax.fori_loop(0, n_full, att_pair, 0)

    @pl.when(npair % 2 == 1)
    def _():
        softmax_chunk(lga_ref, mxa_ref, pa_ref, corra_ref)
        pv_update(last, pa_ref, corra_ref)

    outs = []
    for hd in range(N_HEADS):
        lanes = slice(hd * LANES, (hd + 1) * LANES)
        o = (ot_ref[0:KV_RANK, lanes] / ot_ref[KV_RANK:KV_RANK + 1, lanes]).astype(jnp.bfloat16)
        outs.append(_dot(wuvt_ref[hd], o))
    att_t = jnp.concatenate(outs, axis=0)
    att_ref[0] = att_t.T.astype(jnp.bfloat16)


def _post_kernel(x_ref, att_ref, sig_ref, mconv_ref, p_ref, lng_ref, lnb_ref, wbra_ref, wo_ref,
                 ln1g_ref, ln1b_ref, wup_ref, cw_ref, wdown_ref, wpg_ref, bpg_ref, wple_ref,
                 ln2g_ref, ln2b_ref,
                 out_ref,
                 prea_ref, preb_ref, carry_ref, h1b_ref, act_ref, acc_ref):
    tm = x_ref.shape[1]
    n_pairs = N_FF_CHUNKS // 2

    @pl.when(pl.program_id(1) == 0)
    def _():
        carry_ref[...] = jnp.zeros(carry_ref.shape, jnp.float32)

    h = _layer_norm(x_ref[0], lng_ref[...], lnb_ref[...])
    merged = (sig_ref[0].astype(jnp.float32) * _dot(att_ref[0], wbra_ref[...])
              + mconv_ref[0].astype(jnp.float32))
    y = _dot(merged.astype(jnp.bfloat16), wo_ref[...])
    h1 = _layer_norm(ALPHA * h + y, ln1g_ref[...], ln1b_ref[...])
    h1b_ref[...] = h1.astype(jnp.bfloat16)
    ple = (jax.nn.sigmoid(_dot(h1b_ref[...], wpg_ref[...]) + bpg_ref[...])
           * _dot(p_ref[0, 0].astype(jnp.bfloat16), wple_ref[...]))

    def up_dot(j):
        return _dot(h1b_ref[...], wup_ref[j])

    def conv_act(pre_ref, j):
        pre_ref[0:SUBLANES, :] = carry_ref[j]
        cw = cw_ref[j]
        pre = pre_ref[SUBLANES:SUBLANES + tm, :]
        conv = (cw[0:1, :] * pre_ref[SUBLANES - 2:SUBLANES - 2 + tm, :]
                + cw[1:2, :] * pre_ref[SUBLANES - 1:SUBLANES - 1 + tm, :]
                + cw[2:3, :] * pre + cw[3:4, :])
        carry_ref[j] = pre[tm - SUBLANES:tm, :]
        g = conv[:, :FF_CHUNK]
        u = conv[:, FF_CHUNK:]
        return (g * jax.nn.sigmoid(g) * u).astype(jnp.bfloat16)

    prea_ref[SUBLANES:SUBLANES + tm, :] = up_dot(0)
    act_ref[...] = jnp.zeros(act_ref.shape, jnp.bfloat16)
    acc_ref[...] = jnp.zeros(acc_ref.shape, jnp.float32)

    def ff_pair(i, carry):
        j0 = 2 * i
        r0 = pl.multiple_of(jnp.maximum(i - 1, 0) * (2 * FF_CHUNK), 2 * FF_CHUNK)
        acc_ref[...] += _dot(act_ref[...], wdown_ref[pl.ds(r0, 2 * FF_CHUNK), :])
        preb_ref[SUBLANES:SUBLANES + tm, :] = up_dot(j0 + 1)
        act_ref[:, 0:FF_CHUNK] = conv_act(prea_ref, j0)
        prea_ref[SUBLANES:SUBLANES + tm, :] = up_dot(jnp.minimum(j0 + 2, N_FF_CHUNKS - 1))
        act_ref[:, FF_CHUNK:2 * FF_CHUNK] = conv_act(preb_ref, j0 + 1)
        return carry

    lax.fori_loop(0, n_pairs, ff_pair, 0)
    ffn = acc_ref[...] + _dot(act_ref[...], wdown_ref[(n_pairs - 1) * 2 * FF_CHUNK:n_pairs * 2 * FF_CHUNK, :])
    if N_FF_CHUNKS % 2:
        ffn = ffn + _dot(conv_act(prea_ref, N_FF_CHUNKS - 1), wdown_ref[n_pairs * 2 * FF_CHUNK:D_FF, :])

    out_ref[0] = _layer_norm(ALPHA * h1 + ffn + ple, ln2g_ref[...], ln2b_ref[...])


def _const_spec(shape):
    nd = len(shape)
    return pl.BlockSpec(shape, lambda *_: (0,) * nd, pipeline_mode=pl.Buffered(1))


def kernel(x, p, ln_emb_g, ln_emb_b, w_in, b_gate, kv_norm_g, w_uk, w_uv, k_idx_ln_g, k_idx_ln_b,
           mix_conv_w, mix_conv_b, w_br_att, w_br_conv, w_o, ln1_g, ln1_b, w_ffn_up, ffn_conv_w,
           ffn_conv_b, w_ffn_down, w_ple_gate, b_ple_gate, w_ple, ln2_g, ln2_b):
    bsz, seq, _ = x.shape
    assert seq % TM_PRE == 0 and seq % TM_POST == 0 and TM_PRE % (2 * ATT_CHUNK) == 0
    nb = seq // Q_BLOCK
    topk = min(TOPK_MAX, seq // 4)
    idx_bits = max(1, (seq - 1).bit_length())
    bf16, f32 = jnp.bfloat16, jnp.float32
    i = 0

    o_qatt = 0
    o_ckv = o_qatt + ATT_WIDTH
    o_qidx = o_ckv + KV_RANK
    o_kidx = o_qidx + N_IDX_HEADS * IDX_DIM
    o_widx = o_kidx + IDX_DIM
    o_cvb = o_widx + N_IDX_HEADS
    win = w_in[i]
    pad64 = jnp.zeros((D_MODEL, LANES - IDX_DIM), f32)
    w_tok = jnp.concatenate(
        [win[:, o_ckv:o_ckv + KV_RANK], win[:, o_kidx:o_kidx + IDX_DIM], pad64, win[:, o_cvb:]],
        axis=1).astype(bf16)
    w_qt = jnp.concatenate([win[:, o_qatt:o_qatt + ATT_WIDTH],
                            win[:, o_qidx:o_qidx + N_IDX_HEADS * IDX_DIM]], axis=1).T.astype(bf16)
    w_wt = jnp.concatenate([win[:, o_widx:o_widx + N_IDX_HEADS].T,
                            jnp.zeros((SUBLANES_16 - N_IDX_HEADS, D_MODEL), f32)], axis=0).astype(bf16)
    row = lambda v: v.reshape(1, -1).astype(f32)
    kig = jnp.concatenate([k_idx_ln_g[i], jnp.zeros((LANES - IDX_DIM,), f32)]).reshape(1, LANES)
    kib = jnp.concatenate([k_idx_ln_b[i], jnp.zeros((LANES - IDX_DIM,), f32)]).reshape(1, LANES)
    cw_mix = jnp.concatenate([mix_conv_w[i], mix_conv_b[i][None, :],
                              jnp.zeros((SUBLANES - 4, D_CONV), f32)], axis=0)

    n_pre = seq // TM_PRE
    sub_pre = TM_PRE // Q_BLOCK
    tok3 = lambda w: pl.BlockSpec((1, TM_PRE, w), lambda b, t: (b, t, 0))
    blk4 = lambda r, w: pl.BlockSpec((1, sub_pre, r, w), lambda b, t: (b, t, 0, 0))
    pre_out_shapes = (
        jax.ShapeDtypeStruct((bsz, seq, KV_RANK), bf16),
        jax.ShapeDtypeStruct((bsz, seq // ATT_CHUNK, OT_ROWS, ATT_CHUNK), bf16),
        jax.ShapeDtypeStruct((bsz, seq, IDX_DIM), bf16),
        jax.ShapeDtypeStruct((bsz, nb, KV_RANK, N_HEADS * LANES), bf16),
        jax.ShapeDtypeStruct((bsz, nb, IDX_DIM, N_IDX_HEADS * LANES), bf16),
        jax.ShapeDtypeStruct((bsz, nb, N_IDX_HEADS, Q_BLOCK), f32),
        jax.ShapeDtypeStruct((bsz, seq, D_MODEL), bf16),
        jax.ShapeDtypeStruct((bsz, seq, D_MODEL), bf16),
    )
    ckv, ckvt, kidx, qlatt, qidxt, wt, mconv, sig = pl.pallas_call(
        _pre_kernel,
        grid=(bsz, n_pre),
        in_specs=[
            tok3(D_MODEL),
            _const_spec((1, D_MODEL)), _const_spec((1, D_MODEL)),
            _const_spec((D_MODEL, _TOK_WIDTH)), _const_spec((2 * ATT_WIDTH, D_MODEL)),
            _const_spec((SUBLANES_16, D_MODEL)),
            _const_spec((1, KV_RANK)), _const_spec((1, LANES)), _const_spec((1, LANES)),
            _const_spec((N_HEADS, KV_RANK, HEAD_DIM)),
            _const_spec((SUBLANES, D_CONV)), _const_spec((D_CONV, D_MODEL)), _const_spec((2, D_MODEL)),
        ],
        out_specs=(
            tok3(KV_RANK),
            pl.BlockSpec((1, TM_PRE // ATT_CHUNK, OT_ROWS, ATT_CHUNK), lambda b, t: (b, t, 0, 0)),
            tok3(IDX_DIM),
            blk4(KV_RANK, N_HEADS * LANES), blk4(IDX_DIM, N_IDX_HEADS * LANES),
            blk4(N_IDX_HEADS, Q_BLOCK), tok3(D_MODEL), tok3(D_MODEL),
        ),
        out_shape=pre_out_shapes,
        scratch_shapes=[pltpu.VMEM((TM_PRE + SUBLANES, D_CONV), f32)],
        compiler_params=pltpu.CompilerParams(
            dimension_semantics=("arbitrary", "arbitrary"), vmem_limit_bytes=VMEM_LIMIT),
    )(x, row(ln_emb_g), row(ln_emb_b), w_tok, w_qt, w_wt, row(kv_norm_g[i]), kig, kib,
      w_uk[i].astype(bf16), cw_mix, w_br_conv[i].astype(bf16), b_gate[i].astype(f32))

    w_uvt = jnp.swapaxes(w_uv[i], 1, 2).astype(bf16)
    att = pl.pallas_call(
        functools.partial(_attn_kernel, topk=topk, idx_bits=idx_bits),
        grid=(bsz, nb),
        in_specs=[
            pl.BlockSpec((1, 1, KV_RANK, N_HEADS * LANES), lambda b, q: (b, q, 0, 0)),
            pl.BlockSpec((1, 1, IDX_DIM, N_IDX_HEADS * LANES), lambda b, q: (b, q, 0, 0)),
            pl.BlockSpec((1, 1, N_IDX_HEADS, Q_BLOCK), lambda b, q: (b, q, 0, 0)),
            pl.BlockSpec((1, seq, KV_RANK), lambda b, q: (b, 0, 0)),
            pl.BlockSpec((1, seq // ATT_CHUNK, OT_ROWS, ATT_CHUNK), lambda b, q: (b, 0, 0, 0)),
            pl.BlockSpec((1, seq, IDX_DIM), lambda b, q: (b, 0, 0)),
            _const_spec((N_HEADS, HEAD_DIM, KV_RANK)),
        ],
        out_specs=pl.BlockSpec((1, Q_BLOCK, ATT_WIDTH), lambda b, q: (b, q, 0)),
        out_shape=jax.ShapeDtypeStruct((bsz, seq, ATT_WIDTH), bf16),
        scratch_shapes=[
            pltpu.VMEM((seq, LANES), jnp.int32),
            pltpu.VMEM((seq // 2, LANES), jnp.int32),
            pltpu.VMEM((seq // 2, LANES), jnp.int32),
            pltpu.VMEM((seq, 2 * KV_RANK), bf16),
            pltpu.VMEM((2 * KV_RANK, N_HEADS * LANES), bf16),
            pltpu.VMEM((Q_BLOCK, N_IDX_HEADS * LANES), f32),
            pltpu.VMEM((Q_BLOCK, N_IDX_HEADS * LANES), f32),
            pltpu.VMEM((ATT_CHUNK, N_HEADS * LANES), f32),
            pltpu.VMEM((ATT_CHUNK, N_HEADS * LANES), f32),
            pltpu.VMEM((ATT_CHUNK, N_HEADS * LANES), bf16),
            pltpu.VMEM((ATT_CHUNK, N_HEADS * LANES), bf16),
            pltpu.VMEM((N_HEADS, LANES), f32),
            pltpu.VMEM((N_HEADS, LANES), f32),
            pltpu.VMEM((SUBLANES, N_HEADS * LANES), f32),
            pltpu.VMEM((SUBLANES, N_HEADS * LANES), f32),
            pltpu.VMEM((OT_ROWS, N_HEADS * LANES), f32),
            pltpu.VMEM((N_HEADS, LANES), f32),
            pltpu.VMEM((1, LANES), jnp.int32),
        ],
        compiler_params=pltpu.CompilerParams(
            dimension_semantics=("arbitrary", "arbitrary"), vmem_limit_bytes=VMEM_LIMIT),
    )(qlatt, qidxt, wt, ckv, ckvt, kidx, w_uvt)

    wup = w_ffn_up[i]
    wup_c = jnp.concatenate(
        [wup[:, :D_FF].reshape(D_MODEL, N_FF_CHUNKS, FF_CHUNK),
         wup[:, D_FF:].reshape(D_MODEL, N_FF_CHUNKS, FF_CHUNK)], axis=2)
    wup_c = jnp.transpose(wup_c, (1, 0, 2)).astype(bf16)
    cwf = jnp.concatenate([ffn_conv_w[i], ffn_conv_b[i][None, :],
                           jnp.zeros((SUBLANES - 4, 2 * D_FF), f32)], axis=0)
    cwf_c = jnp.concatenate(
        [cwf[:, :D_FF].reshape(SUBLANES, N_FF_CHUNKS, FF_CHUNK),
         cwf[:, D_FF:].reshape(SUBLANES, N_FF_CHUNKS, FF_CHUNK)], axis=2)
    cwf_c = jnp.transpose(cwf_c, (1, 0, 2))
    wdown_c = w_ffn_down[i].astype(bf16)

    n_post = seq // TM_POST
    tokp = lambda w: pl.BlockSpec((1, TM_POST, w), lambda b, t: (b, t, 0))
    out = pl.pallas_call(
        _post_kernel,
        grid=(bsz, n_post),
        in_specs=[
            tokp(D_MODEL), tokp(ATT_WIDTH), tokp(D_MODEL), tokp(D_MODEL),
            pl.BlockSpec((1, 1, TM_POST, PLE_DIM), lambda b, t: (0, b, t, 0)),
            _const_spec((1, D_MODEL)), _const_spec((1, D_MODEL)),
            _const_spec((ATT_WIDTH, D_MODEL)), _const_spec((D_MODEL, D_MODEL)),
            _const_spec((1, D_MODEL)), _const_spec((1, D_MODEL)),
            _const_spec((N_FF_CHUNKS, D_MODEL, 2 * FF_CHUNK)),
            _const_spec((N_FF_CHUNKS, SUBLANES, 2 * FF_CHUNK)),
            _const_spec((D_FF, D_MODEL)),
            _const_spec((D_MODEL, D_MODEL)), _const_spec((1, D_MODEL)),
            _const_spec((PLE_DIM, D_MODEL)),
            _const_spec((1, D_MODEL)), _const_spec((1, D_MODEL)),
        ],
        out_specs=tokp(D_MODEL),
        out_shape=jax.ShapeDtypeStruct((bsz, seq, D_MODEL), x.dtype),
        scratch_shapes=[
            pltpu.VMEM((TM_POST + SUBLANES, 2 * FF_CHUNK), f32),
            pltpu.VMEM((TM_POST + SUBLANES, 2 * FF_CHUNK), f32),
            pltpu.VMEM((N_FF_CHUNKS, SUBLANES, 2 * FF_CHUNK), f32),
            pltpu.VMEM((TM_POST, D_MODEL), bf16),
            pltpu.VMEM((TM_POST, 2 * FF_CHUNK), bf16),
            pltpu.VMEM((TM_POST, D_MODEL), f32),
        ],
        compiler_params=pltpu.CompilerParams(
            dimension_semantics=("arbitrary", "arbitrary"), vmem_limit_bytes=VMEM_LIMIT),
    )(x, att, sig, mconv, p, row(ln_emb_g), row(ln_emb_b), w_br_att[i].astype(bf16),
      w_o[i].astype(bf16), row(ln1_g[i]), row(ln1_b[i]), wup_c, cwf_c, wdown_c,
      w_ple_gate[i].astype(bf16), row(b_ple_gate[i]), w_ple[i].astype(bf16),
      row(ln2_g[i]), row(ln2_b[i]))
    return out
```
